```python
import jax, jax.numpy as jnp
from jax import lax
import numpy as np

D_MODEL = 2048
BATCH = 4
SEQ = 4096
DEPTH = 4

D_A = D_MODEL // 2
HEAD_SIZE = 64
N_HEADS_A = D_A // HEAD_SIZE
LORA_W = 64
LORA_A = 64
D_B = D_MODEL // 2
CONV_GROUPS = 16
CONV_WIDTH = 3
N_BRANCH = 2
RMS_EPS = 1e-6
GN_EPS = 64e-5

COLS_A = 3 * D_A + LORA_W + LORA_A
OFF_ZA = COLS_A
OFF_B = OFF_ZA + D_A
OFF_GATE = OFF_B + 4 * D_B
N_IN = OFF_GATE + N_BRANCH * D_MODEL

kernel_name = "hybrid_rwkv7_shortconv_gated_merge"


def rms_norm(x, g):
    xf = x.astype(jnp.float32)
    y = xf * lax.rsqrt(jnp.mean(xf * xf, axis=-1, keepdims=True) + RMS_EPS)
    return (y * g.astype(jnp.float32)).astype(x.dtype)


def shift_time(u, n):
    return jnp.pad(u, ((0, 0), (n, 0), (0, 0)))[:, : u.shape[1]]


def wkv7_scan(r, decay, k, v, kk, b):
    def step(S, inp):
        r_t, w_t, k_t, v_t, kk_t, b_t = inp
        sa = jnp.einsum('bhvk,bhk->bhv', S, -kk_t)
        S = (S * w_t[:, :, None, :]
             + sa[..., None] * b_t[:, :, None, :]
             + v_t[..., None] * k_t[:, :, None, :])
        y_t = jnp.einsum('bhvk,bhk->bhv', S, r_t)
        return S, y_t
    xs = (jnp.swapaxes(r, 0, 1), jnp.swapaxes(decay, 0, 1), jnp.swapaxes(k, 0, 1),
          jnp.swapaxes(v, 0, 1), jnp.swapaxes(kk, 0, 1), jnp.swapaxes(b, 0, 1))
    bsz, _, nh, n = r.shape
    S0 = jnp.zeros((bsz, nh, n, n), jnp.float32)
    _, y = lax.scan(step, S0, xs)
    return jnp.swapaxes(y, 0, 1)


def rwkv7_mix(pa, w0, w2, a0, a2, k_k, k_a, r_k, lnx_g, lnx_b):
    bsz, t, _ = pa.shape
    pa = pa.astype(jnp.float32)
    r = pa[..., :D_A]
    k = pa[..., D_A:2 * D_A]
    v = pa[..., 2 * D_A:3 * D_A]
    xw = pa[..., 3 * D_A:3 * D_A + LORA_W]
    xa = pa[..., 3 * D_A + LORA_W:]
    f = lambda p: p.astype(jnp.float32)
    w_log = -jax.nn.softplus(-(f(w0) + jnp.tanh(xw) @ f(w2))) - 0.5
    decay = jnp.exp(-jnp.exp(w_log))
    a = jax.nn.sigmoid(f(a0) + xa @ f(a2))
    hs = (bsz, t, N_HEADS_A, HEAD_SIZE)
    kk = (k * f(k_k)).reshape(hs)
    kk = kk / jnp.maximum(jnp.sqrt(jnp.sum(kk * kk, axis=-1, keepdims=True)), 1e-12)
    k = k * (1.0 + (a - 1.0) * f(k_a))
    r_h, k_h, v_h = r.reshape(hs), k.reshape(hs), v.reshape(hs)
    b_h = kk * a.reshape(hs)
    y = wkv7_scan(r_h, decay.reshape(hs), k_h, v_h, kk, b_h)
    mu = jnp.mean(y, axis=-1, keepdims=True)
    var = jnp.mean(jnp.square(y - mu), axis=-1, keepdims=True)
    y = ((y - mu) * lax.rsqrt(var + GN_EPS)).reshape(bsz, t, D_A) * f(lnx_g) + f(lnx_b)
    bonus = jnp.sum(r_h * k_h * f(r_k), axis=-1, keepdims=True) * v_h
    return y + bonus.reshape(bsz, t, D_A)


def short_conv_mix(pb, conv_w):
    bg = pb[..., :D_B]
    cg = pb[..., D_B:2 * D_B]
    hb = pb[..., 2 * D_B:]
    u = cg * hb
    y = conv_w[0] * shift_time(u, 2) + conv_w[1] * shift_time(u, 1) + conv_w[2] * u
    return bg * y


def setup_inputs(seed: int = 0) -> dict:
    key = jax.random.key(seed)
    ks = jax.random.split(key, 24)
    L, D = DEPTH, D_MODEL
    nrm = lambda k, s, sc: jax.random.normal(k, s, jnp.float32) * sc
    return {
        "x": nrm(ks[0], (BATCH, SEQ, D), 1.0),
        "c": nrm(ks[1], (BATCH, D), 1.0),
        "ada_w": nrm(ks[2], (L, D, 3 * D), 0.5 * D ** -0.5),
        "ada_b": nrm(ks[3], (L, 3 * D), 0.01),
        "pre_gain": 1.0 + nrm(ks[4], (L, D), 0.05),
        "post_gain": 1.0 + nrm(ks[5], (L, D), 0.05),
        "w_in": nrm(ks[6], (L, D, N_IN), D ** -0.5),
        "mu_shift": jax.random.uniform(ks[7], (L, COLS_A), jnp.float32),
        "w0": nrm(ks[8], (L, D_A), 1.0),
        "w2": nrm(ks[9], (L, LORA_W, D_A), 0.5 * LORA_W ** -0.5),
        "a0": nrm(ks[10], (L, D_A), 0.5),
        "a2": nrm(ks[11], (L, LORA_A, D_A), 0.5 * LORA_A ** -0.5),
        "k_k": 0.85 + nrm(ks[12], (L, D_A), 0.05),
        "k_a": 1.0 + nrm(ks[13], (L, D_A), 0.05),
        "r_k": nrm(ks[14], (L, N_HEADS_A, HEAD_SIZE), 0.1),
        "lnx_gain": 1.0 + nrm(ks[15], (L, D_A), 0.05),
        "lnx_bias": nrm(ks[16], (L, D_A), 0.01),
        "conv_w": nrm(ks[17], (L, CONV_WIDTH, D_B), CONV_WIDTH ** -0.5),
        "p_a": nrm(ks[18], (L, D_A, D), D_A ** -0.5),
        "p_b": nrm(ks[19], (L, D_B, D), D_B ** -0.5),
        "w_out": nrm(ks[20], (L, D, D), D ** -0.5),
    }


def reference(x, c, ada_w, ada_b, pre_gain, post_gain, w_in, mu_shift, w0, w2, a0, a2,
              k_k, k_a, r_k, lnx_gain, lnx_bias, conv_w, p_a, p_b, w_out):
    D = D_MODEL
    c_act = jax.nn.silu(c)
    for l in range(DEPTH):
        mod = c_act @ ada_w[l] + ada_b[l]
        shift = mod[:, None, :D]
        scale = mod[:, None, D:2 * D]
        gate = mod[:, None, 2 * D:]
        h = rms_norm(x, pre_gain[l]) * (1.0 + scale) + shift
        proj = h @ w_in[l]
        pa = proj[..., :COLS_A]
        pa = pa + (shift_time(pa, 1) - pa) * mu_shift[l]
        z_a = proj[..., OFF_ZA:OFF_B]
        pb = proj[..., OFF_B:OFF_B + 3 * D_B]
        z_b = proj[..., OFF_B + 3 * D_B:OFF_GATE]
        g_a = proj[..., OFF_GATE:OFF_GATE + D]
        g_b = proj[..., OFF_GATE + D:]
        y_a = rwkv7_mix(pa, w0[l], w2[l], a0[l], a2[l], k_k[l], k_a[l], r_k[l],
                        lnx_gain[l], lnx_bias[l]).astype(x.dtype)
        y_a = (y_a * jax.nn.silu(z_a)) @ p_a[l]
        y_b = (short_conv_mix(pb, conv_w[l]) * jax.nn.silu(z_b)) @ p_b[l]
        m = jax.nn.sigmoid(g_a) * y_a + jax.nn.sigmoid(g_b) * y_b
        o = m @ w_out[l]
        x = x + gate * rms_norm(o, post_gain[l])
    return x
```

```python
import functools
import math

import jax
import jax.numpy as jnp
from jax import lax
from jax.experimental import pallas as pl
from jax.experimental.pallas import tpu as pltpu

HEAD_SIZE = 64
LORA = 64
RMS_EPS = 1e-6
GN_EPS = 64e-5
CHUNK = 64
SUB = 16
LANES = 128
VMEM_LIMIT = 56 * 1024 * 1024

F32 = jnp.float32
BF16 = jnp.bfloat16

(ROW_W0, ROW_A0, ROW_KK, ROW_KA, ROW_RK, ROW_LNG, ROW_LNB, ROW_MUR, ROW_MUK, ROW_MUV,
 ROW_MUL, ROW_CW0, ROW_CW1, ROW_CW2) = range(14)
N_ROWS = 16


def _dot(a, b):
    return jnp.dot(a, b, preferred_element_type=F32)


def _dot_nt(a, b):
    return lax.dot_general(a, b, (((1,), (1,)), ((), ())), preferred_element_type=F32)


def _dot_tn(a, b):
    return lax.dot_general(a, b, (((0,), (0,)), ((), ())), preferred_element_type=F32)


def _split_dot(x, m):
    hi = x.astype(BF16)
    lo = (x - hi.astype(F32)).astype(BF16)
    return _dot(hi, m) + _dot(lo, m)


def _split_dot_left(m, x):
    hi = x.astype(BF16)
    lo = (x - hi.astype(F32)).astype(BF16)
    return _dot(m, hi) + _dot(m, lo)


def _sigmoid(x):
    return 1.0 / (1.0 + jnp.exp(-x))


def _ada_kernel(c_ref, w_ref, b_ref, o_ref):
    c = c_ref[...]
    c_act = c * _sigmoid(c)
    w = w_ref[...]
    w_hi = w.astype(BF16)
    w_lo = (w - w_hi.astype(F32)).astype(BF16)
    o_ref[...] = _split_dot(c_act, w_hi) + _dot(c_act.astype(BF16), w_lo) + b_ref[...]


def _ada_mod(c_pad, ada_w, ada_b3, tn):
    depth, d, n3 = ada_w.shape
    rows = c_pad.shape[0]
    return pl.pallas_call(
        _ada_kernel,
        name="ada_mod",
        grid=(depth, pl.cdiv(n3, tn)),
        in_specs=[
            pl.BlockSpec((rows, d), lambda l, j: (0, 0)),
            pl.BlockSpec((None, d, tn), lambda l, j: (l, 0, j)),
            pl.BlockSpec((None, 1, tn), lambda l, j: (l, 0, j)),
        ],
        out_specs=pl.BlockSpec((None, rows, tn), lambda l, j: (l, 0, j)),
        out_shape=jax.ShapeDtypeStruct((depth, rows, n3), F32),
        compiler_params=pltpu.CompilerParams(
            dimension_semantics=("arbitrary", "arbitrary"), vmem_limit_bytes=VMEM_LIMIT),
    )(c_pad, ada_w, ada_b3)


def _proj_kernel(x_ref, mod_ref, g_ref, w_ref, o_ref, h_ref):
    @pl.when(pl.program_id(1) == 0)
    def _():
        x = x_ref[...]
        ms = jnp.mean(x * x, axis=-1, keepdims=True)
        y = x * lax.rsqrt(ms + RMS_EPS) * g_ref[0:1, :]
        h = y * (1.0 + mod_ref[1:2, :]) + mod_ref[0:1, :]
        h_ref[...] = h.astype(BF16)

    o_ref[...] = _dot(h_ref[...], w_ref[...]).astype(BF16)


def _proj(x2, mod4, gvec, w_in_p, layer, seq, tm, tn):
    bt, d = x2.shape
    n_in = w_in_p.shape[-1]
    rows_per_seq = seq // tm
    return pl.pallas_call(
        _proj_kernel,
        name="in_proj",
        grid=(bt // tm, pl.cdiv(n_in, tn)),
        in_specs=[
            pl.BlockSpec((tm, d), lambda i, j: (i, 0)),
            pl.BlockSpec((None, None, 3, d), lambda i, j: (layer, i // rows_per_seq, 0, 0)),
            pl.BlockSpec((None, 8, d), lambda i, j: (layer, 0, 0)),
            pl.BlockSpec((None, d, tn), lambda i, j: (layer, 0, j)),
        ],
        out_specs=pl.BlockSpec((tm, tn), lambda i, j: (i, j)),
        out_shape=jax.ShapeDtypeStruct((bt, n_in), BF16),
        scratch_shapes=[pltpu.VMEM((tm, d), BF16)],
        compiler_params=pltpu.CompilerParams(
            dimension_semantics=("parallel", "arbitrary"), vmem_limit_bytes=VMEM_LIMIT),
    )(x2, mod4, gvec, w_in_p)


def _wkv_kernel(r_ref, k_ref, v_ref, lx_ref, z_ref, pv_ref, wl_ref, o_ref,
                s_ref, pr_ref, pk_ref, pv_prev_ref, pl_ref, lora_ref):
    c_idx = pl.program_id(1)
    C = r_ref.shape[0]
    d_a = r_ref.shape[1]
    n_pairs = d_a // LANES

    @pl.when(c_idx == 0)
    def _():
        s_ref[...] = jnp.zeros_like(s_ref)
        pr_ref[...] = jnp.zeros_like(pr_ref)
        pk_ref[...] = jnp.zeros_like(pk_ref)
        pv_prev_ref[...] = jnp.zeros_like(pv_prev_ref)
        pl_ref[...] = jnp.zeros_like(pl_ref)

    row = lax.broadcasted_iota(jnp.int32, (C, LANES), 0)
    lane = lax.broadcasted_iota(jnp.int32, (C, LANES), 1)
    col = lane & (HEAD_SIZE - 1)
    head0 = lane < HEAD_SIZE
    strict = col < row
    incl = col <= row
    same_sub = (col // SUB) == (row // SUB)
    eye = jnp.where(col == row, 1.0, 0.0).astype(F32)
    first_row = row == 0

    sq_r = lax.broadcasted_iota(jnp.int32, (LANES, LANES), 0)
    sq_c = lax.broadcasted_iota(jnp.int32, (LANES, LANES), 1)
    same_head = (sq_r < HEAD_SIZE) == (sq_c < HEAD_SIZE)
    ones_bd = jnp.where(same_head, 1.0, 0.0).astype(BF16)
    tr = lax.broadcasted_iota(jnp.int32, (C, C), 0)
    tc = lax.broadcasted_iota(jnp.int32, (C, C), 1)
    tril = jnp.where(tc <= tr, 1.0, 0.0).astype(BF16)

    def shifted(x, prev):
        return jnp.where(first_row[:, : x.shape[1]], prev, pltpu.roll(x, 1, 0))

    def stack2(x):
        z = jnp.zeros_like(x)
        return jnp.concatenate([jnp.where(head0, x, z), jnp.where(head0, z, x)], axis=0)

    def mm(x, y_stacked):
        return _dot(x.astype(BF16), y_stacked)

    lx = lx_ref[...].astype(F32)
    lx_s = lx + (shifted(lx, pl_ref[...]) - lx) * pv_ref[ROW_MUL:ROW_MUL + 1, 0:LANES]
    lora_in = jnp.where(head0, jnp.tanh(lx_s), lx_s).astype(BF16)
    lora_ref[...] = _dot(lora_in, wl_ref[...])

    for hp in range(n_pairs):
        sl = slice(hp * LANES, (hp + 1) * LANES)
        par = lambda r_: pv_ref[r_:r_ + 1, sl]

        r_raw = r_ref[:, sl].astype(F32)
        k_raw = k_ref[:, sl].astype(F32)
        v_raw = v_ref[:, sl].astype(F32)
        r = r_raw + (shifted(r_raw, pr_ref[:, sl]) - r_raw) * par(ROW_MUR)
        k = k_raw + (shifted(k_raw, pk_ref[:, sl]) - k_raw) * par(ROW_MUK)
        v = v_raw + (shifted(v_raw, pv_prev_ref[:, sl]) - v_raw) * par(ROW_MUV)

        lw = lora_ref[:, sl]
        la = lora_ref[:, d_a + hp * LANES: d_a + (hp + 1) * LANES]
        logw = (-math.exp(-0.5)) * _sigmoid(par(ROW_W0) + lw)
        lr = _sigmoid(par(ROW_A0) + la)

        kk = k * par(ROW_KK)
        ss = _split_dot(kk * kk, ones_bd)
        kk = kk / jnp.maximum(jnp.sqrt(ss), 1e-12)
        kmod = k * (1.0 + (lr - 1.0) * par(ROW_KA))
        bvec = kk * lr
        bonus = _split_dot(r * kmod * par(ROW_RK), ones_bd) * v

        p_inc = _split_dot_left(tril, logw)
        p_mid = p_inc[C // 2 - 1: C // 2, :]
        p_end = p_inc[C - 1: C, :]
        pc = p_inc - p_mid
        e_in = jnp.exp(pc)
        a_t = -kk * jnp.exp(pc - logw)
        r_t = r * e_in
        e_inv = jnp.exp(-pc)
        b_t = bvec * e_inv
        k_t = kmod * e_inv
        e_out = jnp.exp(p_end - p_inc)
        w_mid = jnp.exp(p_mid)
        w_end = jnp.exp(p_end)

        ar = jnp.concatenate([a_t, r_t], axis=0).astype(BF16)
        bk = jnp.concatenate([stack2(b_t.astype(BF16)), stack2(k_t.astype(BF16))], axis=0)
        a_all = _dot_nt(ar, bk)
        a_ab = jnp.where(strict, a_all[:C, :LANES], 0.0)
        a_ak = jnp.where(strict, a_all[:C, LANES:], 0.0)
        a_rb = jnp.where(incl, a_all[C:, :LANES], 0.0)
        a_rk = jnp.where(incl, a_all[C:, LANES:], 0.0)

        l_d = jnp.where(same_sub, a_ab, 0.0)
        l_o = a_ab - l_d
        l_d2 = mm(l_d, stack2(l_d.astype(BF16)))
        p1 = eye + l_d
        x = mm(jnp.concatenate([p1, l_d2], axis=0), stack2(l_d2.astype(BF16)))
        p2 = p1 + x[:C]
        l_d4 = x[C:]
        x = mm(jnp.concatenate([p2, l_d4], axis=0), stack2(l_d4.astype(BF16)))
        p3 = p2 + x[:C]
        l_d8 = x[C:]
        t_d = p3 + mm(p3, stack2(l_d8.astype(BF16)))
        n1 = mm(t_d, stack2(l_o.astype(BF16)))
        n2 = mm(n1, stack2(n1.astype(BF16)))
        q = (eye + n1) + mm(eye + n1, stack2(n2.astype(BF16)))
        t_inv = mm(q, stack2(t_d.astype(BF16)))

        s_prev = s_ref[hp]
        s_hat = (s_prev * w_mid).astype(BF16)
        ar_s = _dot_nt(ar, s_hat)
        v_b = v.astype(BF16)
        v_st = stack2(v_b)
        rhs = ar_s[:C] + mm(a_ak, v_st)
        u = mm(t_inv, stack2(rhs.astype(BF16)))
        u_b = u.astype(BF16)
        y = ar_s[C:] + mm(jnp.concatenate([a_rb, a_rk], axis=1),
                          jnp.concatenate([stack2(u_b), v_st], axis=0))
        uv = jnp.concatenate([u_b, v_b], axis=0)
        bk_d = jnp.concatenate([bvec * e_out, kmod * e_out], axis=0).astype(BF16)
        s_new = s_prev * w_end + _dot_tn(uv, bk_d)
        s_ref[hp] = jnp.where(same_head, s_new, 0.0)

        mu = _split_dot(y, ones_bd) * (1.0 / HEAD_SIZE)
        yc = y - mu
        var = _split_dot(yc * yc, ones_bd) * (1.0 / HEAD_SIZE)
        yn = yc * lax.rsqrt(var + GN_EPS) * par(ROW_LNG) + par(ROW_LNB)
        z = z_ref[:, sl].astype(F32)
        o_ref[:, sl] = ((yn + bonus) * (z * _sigmoid(z))).astype(BF16)

    pr_ref[...] = r_ref[C - 1:C, :].astype(F32)
    pk_ref[...] = k_ref[C - 1:C, :].astype(F32)
    pv_prev_ref[...] = v_ref[C - 1:C, :].astype(F32)
    pl_ref[...] = lx[C - 1:C, :]


def _wkv(proj3, pvec, wl, layer, d_a):
    bsz, seq, _ = proj3.shape
    C = CHUNK
    n_pairs = d_a // LANES
    col = lambda idx: (lambda b, c: (b, c, idx))
    return pl.pallas_call(
        _wkv_kernel,
        name="wkv7_chunked",
        grid=(bsz, seq // C),
        in_specs=[
            pl.BlockSpec((None, C, d_a), col(9)),
            pl.BlockSpec((None, C, d_a), col(10)),
            pl.BlockSpec((None, C, d_a), col(11)),
            pl.BlockSpec((None, C, LANES), col(12 * d_a // LANES)),
            pl.BlockSpec((None, C, d_a), col(8)),
            pl.BlockSpec((None, N_ROWS, d_a), lambda b, c: (layer, 0, 0)),
            pl.BlockSpec((None, LANES, 2 * d_a), lambda b, c: (layer, 0, 0)),
        ],
        out_specs=pl.BlockSpec((None, C, d_a), lambda b, c: (b, c, 0)),
        out_shape=jax.ShapeDtypeStruct((bsz, seq, d_a), BF16),
        scratch_shapes=[
            pltpu.VMEM((n_pairs, LANES, LANES), F32),
            pltpu.VMEM((1, d_a), F32),
            pltpu.VMEM((1, d_a), F32),
            pltpu.VMEM((1, d_a), F32),
            pltpu.VMEM((1, LANES), F32),
            pltpu.VMEM((C, 2 * d_a), F32),
        ],
        compiler_params=pltpu.CompilerParams(
            dimension_semantics=("parallel", "arbitrary"), vmem_limit_bytes=VMEM_LIMIT),
    )(proj3, proj3, proj3, proj3, proj3, pvec, wl)


def _out_kernel(x_ref, ya_ref, ga_ref, gb_ref, bg_ref, cg_ref, hb_ref, zb_ref, cgh_ref, hbh_ref,
                mod_ref, g_ref, pv_ref, pa_ref, pb_ref, wo_ref, o_ref, *, tiles_per_seq):
    tm = x_ref.shape[0]
    i = pl.program_id(0)
    starts_seq = (i % tiles_per_seq) == 0

    u = cg_ref[...].astype(F32) * hb_ref[...].astype(F32)
    halo = cgh_ref[...].astype(F32) * hbh_ref[...].astype(F32)
    halo = jnp.where(starts_seq, 0.0, halo)
    row = lax.broadcasted_iota(jnp.int32, u.shape, 0)
    u1 = jnp.where(row == 0, halo[7:8, :], pltpu.roll(u, 1, 0))
    u2 = jnp.where(row == 0, halo[6:7, :], jnp.where(row == 1, halo[7:8, :], pltpu.roll(u, 2, 0)))
    conv = (pv_ref[ROW_CW0:ROW_CW0 + 1, :] * u2 + pv_ref[ROW_CW1:ROW_CW1 + 1, :] * u1
            + pv_ref[ROW_CW2:ROW_CW2 + 1, :] * u)
    zb = zb_ref[...].astype(F32)
    yb_in = bg_ref[...].astype(F32) * conv * (zb * _sigmoid(zb))

    y_a = _dot(ya_ref[...], pa_ref[...])
    y_b = _dot(yb_in.astype(BF16), pb_ref[...])
    m = _sigmoid(ga_ref[...].astype(F32)) * y_a + _sigmoid(gb_ref[...].astype(F32)) * y_b
    o = _dot(m.astype(BF16), wo_ref[...])
    ms = jnp.mean(o * o, axis=-1, keepdims=True)
    o_n = o * lax.rsqrt(ms + RMS_EPS) * g_ref[1:2, :]
    o_ref[...] = x_ref[...] + mod_ref[2:3, :] * o_n


def _out_stage(x2, ya2, proj, mod4, gvec, pvec, p_a, p_b, w_out, layer, seq, tm, d_a):
    bt, d = x2.shape
    tiles_per_seq = seq // tm
    hb8 = tm // 8
    const = lambda i: (layer, 0, 0)
    colblk = lambda idx: (lambda i: (i, idx))
    halo = lambda idx: (lambda i: (jnp.maximum(i * hb8 - 1, 0), idx))
    single = pl.Buffered(1)
    return pl.pallas_call(
        functools.partial(_out_kernel, tiles_per_seq=tiles_per_seq),
        name="out_stage",
        grid=(bt // tm,),
        in_specs=[
            pl.BlockSpec((tm, d), lambda i: (i, 0)),
            pl.BlockSpec((tm, d_a), lambda i: (i, 0)),
            pl.BlockSpec((tm, d), colblk(0)),
            pl.BlockSpec((tm, d), colblk(1)),
            pl.BlockSpec((tm, d_a), colblk(4)),
            pl.BlockSpec((tm, d_a), colblk(5)),
            pl.BlockSpec((tm, d_a), colblk(6)),
            pl.BlockSpec((tm, d_a), colblk(7)),
            pl.BlockSpec((8, d_a), halo(5)),
            pl.BlockSpec((8, d_a), halo(6)),
            pl.BlockSpec((None, None, 3, d), lambda i: (layer, i // tiles_per_seq, 0, 0)),
            pl.BlockSpec((None, 8, d), const),
            pl.BlockSpec((None, N_ROWS, d_a), const),
            pl.BlockSpec((None, d_a, d), const, pipeline_mode=single),
            pl.BlockSpec((None, d_a, d), const, pipeline_mode=single),
            pl.BlockSpec((None, d, d), const, pipeline_mode=single),
        ],
        out_specs=pl.BlockSpec((tm, d), lambda i: (i, 0)),
        out_shape=jax.ShapeDtypeStruct((bt, d), F32),
        compiler_params=pltpu.CompilerParams(
            dimension_semantics=("parallel",), vmem_limit_bytes=VMEM_LIMIT),
    )(x2, ya2, proj, proj, proj, proj, proj, proj, proj, proj, mod4, gvec, pvec, p_a, p_b, w_out)


def kernel(x, c, ada_w, ada_b, pre_gain, post_gain, w_in, mu_shift, w0, w2, a0, a2, k_k, k_a, r_k,
           lnx_gain, lnx_bias, conv_w, p_a, p_b, w_out):
    bsz, seq, d = x.shape
    depth = ada_w.shape[0]
    d_a = w0.shape[-1]
    cols_a = 3 * d_a + 2 * LORA
    off_b = cols_a + d_a
    off_gate = off_b + 4 * d_a
    assert d == 2 * d_a and d_a % LANES == 0 and seq % CHUNK == 0 and 2 * LORA == LANES
    assert w2.shape[1] == LORA and a2.shape[1] == LORA

    w_in_p = jnp.concatenate(
        [w_in[:, :, off_gate:], w_in[:, :, off_b:off_gate], w_in[:, :, cols_a:off_b],
         w_in[:, :, :cols_a]], axis=-1).astype(BF16)
    pad_l = jnp.zeros((depth, d_a - 2 * LORA), F32)
    rows = [w0, a0, k_k, k_a, r_k.reshape(depth, d_a), lnx_gain, lnx_bias,
            mu_shift[:, :d_a], mu_shift[:, d_a:2 * d_a], mu_shift[:, 2 * d_a:3 * d_a],
            jnp.concatenate([mu_shift[:, 3 * d_a:], pad_l], axis=-1),
            conv_w[:, 0], conv_w[:, 1], conv_w[:, 2]]
    rows += [jnp.zeros((depth, d_a), F32)] * (N_ROWS - len(rows))
    pvec = jnp.stack(rows, axis=1)
    gvec = jnp.stack([pre_gain, post_gain] + [jnp.zeros_like(pre_gain)] * 6, axis=1)
    zl = jnp.zeros((depth, LORA, d_a), F32)
    wl = jnp.concatenate([jnp.concatenate([w2, zl], axis=-1),
                          jnp.concatenate([zl, a2], axis=-1)], axis=1).astype(BF16)
    p_a_b, p_b_b, w_out_b = p_a.astype(BF16), p_b.astype(BF16), w_out.astype(BF16)

    c_rows = 8 * pl.cdiv(bsz, 8)
    c_pad = jnp.zeros((c_rows, d), F32).at[:bsz].set(c)
    mod = _ada_mod(c_pad, ada_w, ada_b.reshape(depth, 1, 3 * d), tn=min(512, 3 * d))
    mod4 = mod[:, :bsz].reshape(depth, bsz, 3, d)

    tm_proj = min(1024, seq)
    tm_out = min(256, seq)
    x2 = x.reshape(bsz * seq, d)
    for layer in range(depth):
        proj = _proj(x2, mod4, gvec, w_in_p, layer, seq, tm_proj, tn=512)
        ya = _wkv(proj.reshape(bsz, seq, -1), pvec, wl, layer, d_a)
        x2 = _out_stage(x2, ya.reshape(bsz * seq, d_a), proj, mod4, gvec, pvec,
                        p_a_b, p_b_b, w_out_b, layer, seq, tm_out, d_a)
    return x2.reshape(bsz, seq, d)
```

```python
import functools
import math

import jax
import jax.numpy as jnp
from jax import lax
from jax.experimental import pallas as pl
from jax.experimental.pallas import tpu as pltpu

HEAD_SIZE = 64
LORA = 64
RMS_EPS = 1e-6
GN_EPS = 64e-5
CHUNK = 64
SUB = 16
LANES = 128
VMEM_LIMIT = 56 * 1024 * 1024

F32 = jnp.float32
BF16 = jnp.bfloat16

(ROW_W0, ROW_A0, ROW_KK, ROW_KA, ROW_RK, ROW_LNG, ROW_LNB, ROW_MUR, ROW_MUK, ROW_MUV,
 ROW_MUL, ROW_CW0, ROW_CW1, ROW_CW2) = range(14)
N_ROWS = 16


def _dot(a, b):
    return jnp.dot(a, b, preferred_element_type=F32)


def _dot_nt(a, b):
    return lax.dot_general(a, b, (((1,), (1,)), ((), ())), preferred_element_type=F32)


def _dot_tn(a, b):
    return lax.dot_general(a, b, (((0,), (0,)), ((), ())), preferred_element_type=F32)


def _split_dot(x, m):
    hi = x.astype(BF16)
    lo = (x - hi.astype(F32)).astype(BF16)
    return _dot(hi, m) + _dot(lo, m)


def _split_dot_left(m, x):
    hi = x.astype(BF16)
    lo = (x - hi.astype(F32)).astype(BF16)
    return _dot(m, hi) + _dot(m, lo)


def _sigmoid(x):
    return 1.0 / (1.0 + jnp.exp(-x))


def _ada_kernel(c_ref, w_ref, b_ref, o_ref):
    c = c_ref[...]
    c_act = c * _sigmoid(c)
    w = w_ref[...]
    w_hi = w.astype(BF16)
    w_lo = (w - w_hi.astype(F32)).astype(BF16)
    o_ref[...] = _split_dot(c_act, w_hi) + _dot(c_act.astype(BF16), w_lo) + b_ref[...]


def _ada_mod(c_pad, ada_w, ada_b3, tn):
    depth, d, n3 = ada_w.shape
    rows = c_pad.shape[0]
    return pl.pallas_call(
        _ada_kernel,
        name="ada_mod",
        grid=(depth, pl.cdiv(n3, tn)),
        in_specs=[
            pl.BlockSpec((rows, d), lambda l, j: (0, 0)),
            pl.BlockSpec((None, d, tn), lambda l, j: (l, 0, j)),
            pl.BlockSpec((None, 1, tn), lambda l, j: (l, 0, j)),
        ],
        out_specs=pl.BlockSpec((None, rows, tn), lambda l, j: (l, 0, j)),
        out_shape=jax.ShapeDtypeStruct((depth, rows, n3), F32),
        compiler_params=pltpu.CompilerParams(
            dimension_semantics=("arbitrary", "arbitrary"), vmem_limit_bytes=VMEM_LIMIT),
    )(c_pad, ada_w, ada_b3)


def _proj_kernel(x_ref, mod_ref, g_ref, w_ref, wl_ref, o_ref, ol_ref, h_ref):
    @pl.when(pl.program_id(1) == 0)
    def _():
        x = x_ref[...]
        ms = jnp.mean(x * x, axis=-1, keepdims=True)
        y = x * lax.rsqrt(ms + RMS_EPS) * g_ref[0:1, :]
        h = (y * (1.0 + mod_ref[1:2, :]) + mod_ref[0:1, :]).astype(BF16)
        h_ref[...] = h
        ol_ref[...] = _dot(h, wl_ref[...]).astype(BF16)

    o_ref[...] = _dot(h_ref[...], w_ref[...]).astype(BF16)


def _proj(x2, mod4, gvec, w_in_b, layer, seq, tm, tn, d_a):
    bt, d = x2.shape
    rows_per_seq = seq // tm
    cols_a = 3 * d_a + 2 * LORA
    off_b = cols_a + d_a
    off_gate = off_b + 4 * d_a
    n_main = 12 * d_a
    assert d_a % tn == 0

    def w_col(j):
        new = j * tn
        old = jnp.where(new < 4 * d_a, new + off_gate,
                        jnp.where(new < 8 * d_a, new + (off_b - 4 * d_a),
                                  jnp.where(new < 9 * d_a, new + (cols_a - 8 * d_a),
                                            new - 9 * d_a)))
        return pl.multiple_of(old, LANES)

    return pl.pallas_call(
        _proj_kernel,
        name="in_proj",
        grid=(bt // tm, n_main // tn),
        in_specs=[
            pl.BlockSpec((tm, d), lambda i, j: (i, 0)),
            pl.BlockSpec((None, None, 3, d), lambda i, j: (layer, i // rows_per_seq, 0, 0)),
            pl.BlockSpec((None, 8, d), lambda i, j: (layer, 0, 0)),
            pl.BlockSpec((pl.Squeezed(), pl.Element(d), pl.Element(tn)),
                         lambda i, j: (layer, 0, w_col(j))),
            pl.BlockSpec((pl.Squeezed(), pl.Element(d), pl.Element(2 * LORA)),
                         lambda i, j: (layer, 0, 3 * d_a)),
        ],
        out_specs=[pl.BlockSpec((tm, tn), lambda i, j: (i, j)),
                   pl.BlockSpec((tm, 2 * LORA), lambda i, j: (i, 0))],
        out_shape=[jax.ShapeDtypeStruct((bt, n_main), BF16),
                   jax.ShapeDtypeStruct((bt, 2 * LORA), BF16)],
        scratch_shapes=[pltpu.VMEM((tm, d), BF16)],
        compiler_params=pltpu.CompilerParams(
            dimension_semantics=("parallel", "arbitrary"), vmem_limit_bytes=VMEM_LIMIT),
    )(x2, mod4, gvec, w_in_b, w_in_b)


def _wkv_kernel(r_ref, k_ref, v_ref, lx_ref, z_ref, pv_ref, wl_ref, o_ref,
                s_ref, pr_ref, pk_ref, pv_prev_ref, pl_ref):
    c_idx = pl.program_id(1)
    C = r_ref.shape[0]
    d_a = r_ref.shape[1]
    n_pairs = d_a // LANES

    @pl.when(c_idx == 0)
    def _():
        s_ref[...] = jnp.zeros_like(s_ref)
        pr_ref[...] = jnp.zeros_like(pr_ref)
        pk_ref[...] = jnp.zeros_like(pk_ref)
        pv_prev_ref[...] = jnp.zeros_like(pv_prev_ref)
        pl_ref[...] = jnp.zeros_like(pl_ref)

    row = lax.broadcasted_iota(jnp.int32, (C, LANES), 0)
    lane = lax.broadcasted_iota(jnp.int32, (C, LANES), 1)
    col = lane & (HEAD_SIZE - 1)
    head0 = lane < HEAD_SIZE
    strict = col < row
    incl = col <= row
    same_sub = (col // SUB) == (row // SUB)
    eye = jnp.where(col == row, 1.0, 0.0).astype(F32)
    first_row = row == 0

    sq_r = lax.broadcasted_iota(jnp.int32, (LANES, LANES), 0)
    sq_c = lax.broadcasted_iota(jnp.int32, (LANES, LANES), 1)
    same_head = (sq_r < HEAD_SIZE) == (sq_c < HEAD_SIZE)
    ones_bd = jnp.where(same_head, 1.0, 0.0).astype(BF16)
    tr = lax.broadcasted_iota(jnp.int32, (C, C), 0)
    tc = lax.broadcasted_iota(jnp.int32, (C, C), 1)
    tril = jnp.where(tc <= tr, 1.0, 0.0).astype(BF16)

    pairs = range(n_pairs)
    sls = [slice(hp * LANES, (hp + 1) * LANES) for hp in pairs]
    first_row_full = lax.broadcasted_iota(jnp.int32, (C, d_a), 0) == 0

    def shifted(x, prev, is_first):
        return jnp.where(is_first, prev, pltpu.roll(x, 1, 0))

    def stack2(x):
        z = jnp.zeros_like(x)
        return jnp.concatenate([jnp.where(head0, x, z), jnp.where(head0, z, x)], axis=0)

    def mm(x, y_stacked):
        return _dot(x.astype(BF16), y_stacked)

    def seg_sum(x):
        st = jnp.concatenate([x[:, s] for s in sls], axis=0)
        out = _dot(st.astype(BF16), ones_bd)
        return jnp.concatenate([out[i * C:(i + 1) * C] for i in pairs], axis=1)

    par = lambda r_: pv_ref[r_:r_ + 1, :]


    lx = lx_ref[...].astype(F32)
    lx_s = lx + (shifted(lx, pl_ref[...], first_row) - lx) * pv_ref[ROW_MUL:ROW_MUL + 1, 0:LANES]
    lora_in = jnp.where(head0, jnp.tanh(lx_s), lx_s).astype(BF16)
    lora = _dot(lora_in, wl_ref[...])

    r_raw = r_ref[...].astype(F32)
    k_raw = k_ref[...].astype(F32)
    v_raw = v_ref[...].astype(F32)
    r = r_raw + (shifted(r_raw, pr_ref[...], first_row_full) - r_raw) * par(ROW_MUR)
    k = k_raw + (shifted(k_raw, pk_ref[...], first_row_full) - k_raw) * par(ROW_MUK)
    v = v_raw + (shifted(v_raw, pv_prev_ref[...], first_row_full) - v_raw) * par(ROW_MUV)

    logw = (-math.exp(-0.5)) * _sigmoid(par(ROW_W0) + lora[:, :d_a])
    lr = _sigmoid(par(ROW_A0) + lora[:, d_a:])

    kk = k * par(ROW_KK)
    kk = kk / jnp.maximum(jnp.sqrt(seg_sum(kk * kk)), 1e-12)
    kmod = k * (1.0 + (lr - 1.0) * par(ROW_KA))
    bvec = kk * lr
    bonus = seg_sum(r * kmod * par(ROW_RK)) * v

    p_inc = _split_dot_left(tril, logw)
    p_mid = p_inc[C // 2 - 1: C // 2, :]
    p_end = p_inc[C - 1: C, :]
    pc = p_inc - p_mid
    a_t = -kk * jnp.exp(pc - logw)
    r_t = r * jnp.exp(pc)
    e_inv = jnp.exp(-pc)
    b_t = (bvec * e_inv).astype(BF16)
    k_t = (kmod * e_inv).astype(BF16)
    e_out = jnp.exp(p_end - p_inc)
    w_mid = jnp.exp(p_mid)
    w_end = jnp.exp(p_end)
    b_d = (bvec * e_out).astype(BF16)
    k_d = (kmod * e_out).astype(BF16)
    v_b = v.astype(BF16)

    ar = [jnp.concatenate([a_t[:, s], r_t[:, s]], axis=0).astype(BF16) for s in sls]
    bk = [jnp.concatenate([stack2(b_t[:, s]), stack2(k_t[:, s])], axis=0) for s in sls]
    a_all = [_dot_nt(ar[i], bk[i]) for i in pairs]

    s_prev = [s_ref[i] for i in pairs]
    s_hat = [(s_prev[i] * w_mid[:, sls[i]]).astype(BF16) for i in pairs]
    ar_s = [_dot_nt(ar[i], s_hat[i]) for i in pairs]
    v_st = [stack2(v_b[:, s]) for s in sls]

    a_ab = [jnp.where(strict, a[:C, :LANES], 0.0) for a in a_all]
    a_ak = [jnp.where(strict, a[:C, LANES:], 0.0) for a in a_all]
    a_r = [jnp.concatenate([jnp.where(incl, a[C:, :LANES], 0.0),
                            jnp.where(incl, a[C:, LANES:], 0.0)], axis=1).astype(BF16)
           for a in a_all]
    rhs = [ar_s[i][:C] + mm(a_ak[i], v_st[i]) for i in pairs]

    l_d = [jnp.where(same_sub, a, 0.0) for a in a_ab]
    l_o = [stack2((a_ab[i] - l_d[i]).astype(BF16)) for i in pairs]
    l_d2 = [mm(l_d[i], stack2(l_d[i].astype(BF16))) for i in pairs]
    p1 = [eye + a for a in l_d]
    x = [mm(jnp.concatenate([p1[i], l_d2[i]], axis=0), stack2(l_d2[i].astype(BF16))) for i in pairs]
    p2 = [p1[i] + x[i][:C] for i in pairs]
    x = [mm(jnp.concatenate([p2[i], x[i][C:]], axis=0), stack2(x[i][C:].astype(BF16))) for i in pairs]
    p3 = [p2[i] + x[i][:C] for i in pairs]
    t_d = [p3[i] + mm(p3[i], stack2(x[i][C:].astype(BF16))) for i in pairs]
    n1 = [mm(t_d[i], l_o[i]) for i in pairs]
    n2 = [mm(n1[i], stack2(n1[i].astype(BF16))) for i in pairs]
    q = [(eye + n1[i]) + mm(eye + n1[i], stack2(n2[i].astype(BF16))) for i in pairs]
    t_inv = [mm(q[i], stack2(t_d[i].astype(BF16))) for i in pairs]

    u_b = [mm(t_inv[i], stack2(rhs[i].astype(BF16))).astype(BF16) for i in pairs]
    y = [ar_s[i][C:] + _dot(a_r[i], jnp.concatenate([stack2(u_b[i]), v_st[i]], axis=0))
         for i in pairs]
    for i in pairs:
        uv = jnp.concatenate([u_b[i], v_b[:, sls[i]]], axis=0)
        bk_d = jnp.concatenate([b_d[:, sls[i]], k_d[:, sls[i]]], axis=0)
        s_new = s_prev[i] * w_end[:, sls[i]] + _dot_tn(uv, bk_d)
        s_ref[i] = jnp.where(same_head, s_new, 0.0)

    y = jnp.concatenate(y, axis=1)
    mu = seg_sum(y) * (1.0 / HEAD_SIZE)
    yc = y - mu
    var = seg_sum(yc * yc) * (1.0 / HEAD_SIZE)
    yn = yc * lax.rsqrt(var + GN_EPS) * par(ROW_LNG) + par(ROW_LNB)
    z = z_ref[...].astype(F32)
    o_ref[...] = ((yn + bonus) * (z * _sigmoid(z))).astype(BF16)

    pr_ref[...] = r_raw[C - 1:C, :]
    pk_ref[...] = k_raw[C - 1:C, :]
    pv_prev_ref[...] = v_raw[C - 1:C, :]
    pl_ref[...] = lx[C - 1:C, :]


def _wkv(proj3, lx3, pvec, wl, layer, d_a):
    bsz, seq, _ = proj3.shape
    C = CHUNK
    n_pairs = d_a // LANES
    col = lambda idx: (lambda b, c: (b, c, idx))
    return pl.pallas_call(
        _wkv_kernel,
        name="wkv7_chunked",
        grid=(bsz, seq // C),
        in_specs=[
            pl.BlockSpec((None, C, d_a), col(9)),
            pl.BlockSpec((None, C, d_a), col(10)),
            pl.BlockSpec((None, C, d_a), col(11)),
            pl.BlockSpec((None, C, LANES), col(0)),
            pl.BlockSpec((None, C, d_a), col(8)),
            pl.BlockSpec((None, N_ROWS, d_a), lambda b, c: (layer, 0, 0)),
            pl.BlockSpec((None, LANES, 2 * d_a), lambda b, c: (layer, 0, 0)),
        ],
        out_specs=pl.BlockSpec((None, C, d_a), lambda b, c: (b, c, 0)),
        out_shape=jax.ShapeDtypeStruct((bsz, seq, d_a), BF16),
        scratch_shapes=[
            pltpu.VMEM((n_pairs, LANES, LANES), F32),
            pltpu.VMEM((1, d_a), F32),
            pltpu.VMEM((1, d_a), F32),
            pltpu.VMEM((1, d_a), F32),
            pltpu.VMEM((1, LANES), F32),
        ],
        compiler_params=pltpu.CompilerParams(
            dimension_semantics=("parallel", "arbitrary"), vmem_limit_bytes=VMEM_LIMIT),
    )(proj3, proj3, proj3, lx3, proj3, pvec, wl)


def _out_kernel(x_ref, ya_ref, ga_ref, gb_ref, bg_ref, cg_ref, hb_ref, zb_ref, cgh_ref, hbh_ref,
                mod_ref, g_ref, pv_ref, pa_ref, pb_ref, wo_ref, o_ref, *, tiles_per_seq):
    tm = x_ref.shape[0]
    i = pl.program_id(0)
    starts_seq = (i % tiles_per_seq) == 0

    u = cg_ref[...].astype(F32) * hb_ref[...].astype(F32)
    halo = cgh_ref[...].astype(F32) * hbh_ref[...].astype(F32)
    halo = jnp.where(starts_seq, 0.0, halo)
    row = lax.broadcasted_iota(jnp.int32, u.shape, 0)
    u1 = jnp.where(row == 0, halo[7:8, :], pltpu.roll(u, 1, 0))
    u2 = jnp.where(row == 0, halo[6:7, :], jnp.where(row == 1, halo[7:8, :], pltpu.roll(u, 2, 0)))
    conv = (pv_ref[ROW_CW0:ROW_CW0 + 1, :] * u2 + pv_ref[ROW_CW1:ROW_CW1 + 1, :] * u1
            + pv_ref[ROW_CW2:ROW_CW2 + 1, :] * u)
    zb = zb_ref[...].astype(F32)
    yb_in = bg_ref[...].astype(F32) * conv * (zb * _sigmoid(zb))

    y_a = _dot(ya_ref[...], pa_ref[...])
    y_b = _dot(yb_in.astype(BF16), pb_ref[...])
    m = _sigmoid(ga_ref[...].astype(F32)) * y_a + _sigmoid(gb_ref[...].astype(F32)) * y_b
    o = _dot(m.astype(BF16), wo_ref[...])
    ms = jnp.mean(o * o, axis=-1, keepdims=True)
    o_n = o * lax.rsqrt(ms + RMS_EPS) * g_ref[1:2, :]
    o_ref[...] = x_ref[...] + mod_ref[2:3, :] * o_n


def _out_stage(x2, ya2, proj, mod4, gvec, pvec, p_a, p_b, w_out, layer, seq, tm, d_a):
    bt, d = x2.shape
    tiles_per_seq = seq // tm
    hb8 = tm // 8
    const = lambda i: (layer, 0, 0)
    colblk = lambda idx: (lambda i: (i, idx))
    halo = lambda idx: (lambda i: (jnp.maximum(i * hb8 - 1, 0), idx))
    single = pl.Buffered(1)
    return pl.pallas_call(
        functools.partial(_out_kernel, tiles_per_seq=tiles_per_seq),
        name="out_stage",
        grid=(bt // tm,),
        in_specs=[
            pl.BlockSpec((tm, d), lambda i: (i, 0)),
            pl.BlockSpec((tm, d_a), lambda i: (i, 0)),
            pl.BlockSpec((tm, d), colblk(0)),
            pl.BlockSpec((tm, d), colblk(1)),
            pl.BlockSpec((tm, d_a), colblk(4)),
            pl.BlockSpec((tm, d_a), colblk(5)),
            pl.BlockSpec((tm, d_a), colblk(6)),
            pl.BlockSpec((tm, d_a), colblk(7)),
            pl.BlockSpec((8, d_a), halo(5)),
            pl.BlockSpec((8, d_a), halo(6)),
            pl.BlockSpec((None, None, 3, d), lambda i: (layer, i // tiles_per_seq, 0, 0)),
            pl.BlockSpec((None, 8, d), const),
            pl.BlockSpec((None, N_ROWS, d_a), const),
            pl.BlockSpec((None, d_a, d), const, pipeline_mode=single),
            pl.BlockSpec((None, d_a, d), const, pipeline_mode=single),
            pl.BlockSpec((None, d, d), const, pipeline_mode=single),
        ],
        out_specs=pl.BlockSpec((tm, d), lambda i: (i, 0)),
        out_shape=jax.ShapeDtypeStruct((bt, d), F32),
        compiler_params=pltpu.CompilerParams(
            dimension_semantics=("parallel",), vmem_limit_bytes=VMEM_LIMIT),
    )(x2, ya2, proj, proj, proj, proj, proj, proj, proj, proj, mod4, gvec, pvec, p_a, p_b, w_out)


def kernel(x, c, ada_w, ada_b, pre_gain, post_gain, w_in, mu_shift, w0, w2, a0, a2, k_k, k_a, r_k,
           lnx_gain, lnx_bias, conv_w, p_a, p_b, w_out):
    bsz, seq, d = x.shape
    depth = ada_w.shape[0]
    d_a = w0.shape[-1]
    assert d == 2 * d_a and d_a % LANES == 0 and seq % CHUNK == 0 and 2 * LORA == LANES
    assert w2.shape[1] == LORA and a2.shape[1] == LORA

    w_in_b = w_in.astype(BF16)
    pad_l = jnp.zeros((depth, d_a - 2 * LORA), F32)
    rows = [w0, a0, k_k, k_a, r_k.reshape(depth, d_a), lnx_gain, lnx_bias,
            mu_shift[:, :d_a], mu_shift[:, d_a:2 * d_a], mu_shift[:, 2 * d_a:3 * d_a],
            jnp.concatenate([mu_shift[:, 3 * d_a:], pad_l], axis=-1),
            conv_w[:, 0], conv_w[:, 1], conv_w[:, 2]]
    rows += [jnp.zeros((depth, d_a), F32)] * (N_ROWS - len(rows))
    pvec = jnp.stack(rows, axis=1)
    gvec = jnp.stack([pre_gain, post_gain] + [jnp.zeros_like(pre_gain)] * 6, axis=1)
    zl = jnp.zeros((depth, LORA, d_a), F32)
    wl = jnp.concatenate([jnp.concatenate([w2, zl], axis=-1),
                          jnp.concatenate([zl, a2], axis=-1)], axis=1).astype(BF16)
    p_a_b, p_b_b, w_out_b = p_a.astype(BF16), p_b.astype(BF16), w_out.astype(BF16)

    c_rows = 8 * pl.cdiv(bsz, 8)
    c_pad = jnp.zeros((c_rows, d), F32).at[:bsz].set(c)
    mod = _ada_mod(c_pad, ada_w, ada_b.reshape(depth, 1, 3 * d), tn=min(512, 3 * d))
    mod4 = mod[:, :bsz].reshape(depth, bsz, 3, d)

    tm_proj = min(1024, seq)
    tm_out = min(256, seq)
    x2 = x.reshape(bsz * seq, d)
    for layer in range(depth):
        proj, lx = _proj(x2, mod4, gvec, w_in_b, layer, seq, tm_proj, min(1024, d_a), d_a)
        ya = _wkv(proj.reshape(bsz, seq, -1), lx.reshape(bsz, seq, -1), pvec, wl, layer, d_a)
        x2 = _out_stage(x2, ya.reshape(bsz * seq, d_a), proj, mod4, gvec, pvec,
                        p_a_b, p_b_b, w_out_b, layer, seq, tm_out, d_a)
    return x2.reshape(bsz, seq, d)
```

```python
import functools
import math

import jax
import jax.numpy as jnp
from jax import lax
from jax.experimental import pallas as pl
from jax.experimental.pallas import tpu as pltpu

HEAD_SIZE = 64
LORA = 64
RMS_EPS = 1e-6
GN_EPS = 64e-5
CHUNK = 64
SUB = 16
LANES = 128
VMEM_LIMIT = 56 * 1024 * 1024
LOG2_E = math.log2(math.e)

F32 = jnp.float32
BF16 = jnp.bfloat16

(ROW_W0, ROW_A0, ROW_KK, ROW_KA, ROW_RK, ROW_LNG, ROW_LNB, ROW_MUR, ROW_MUK, ROW_MUV,
 ROW_MUL, ROW_CW0, ROW_CW1, ROW_CW2) = range(14)
N_ROWS = 16


def _dot(a, b):
    return jnp.dot(a, b, preferred_element_type=F32)


def _dot_nt(a, b):
    return lax.dot_general(a, b, (((1,), (1,)), ((), ())), preferred_element_type=F32)


def _dot_tn(a, b):
    return lax.dot_general(a, b, (((0,), (0,)), ((), ())), preferred_element_type=F32)


def _split_dot(x, m):
    hi = x.astype(BF16)
    lo = (x - hi.astype(F32)).astype(BF16)
    return _dot(hi, m) + _dot(lo, m)


def _split_dot_left(m, x):
    hi = x.astype(BF16)
    lo = (x - hi.astype(F32)).astype(BF16)
    return _dot(m, hi) + _dot(m, lo)


def _sigmoid(x):
    return 1.0 / (1.0 + jnp.exp(-x))


def _ada_kernel(c_ref, w_ref, b_ref, o_ref):
    c = c_ref[...]
    c_act = c * _sigmoid(c)
    w = w_ref[...]
    w_hi = w.astype(BF16)
    w_lo = (w - w_hi.astype(F32)).astype(BF16)
    o_ref[...] = _split_dot(c_act, w_hi) + _dot(c_act.astype(BF16), w_lo) + b_ref[...]


def _ada_mod(c_pad, ada_w, ada_b3, tn):
    depth, d, n3 = ada_w.shape
    rows = c_pad.shape[0]
    return pl.pallas_call(
        _ada_kernel,
        name="ada_mod",
        grid=(depth, pl.cdiv(n3, tn)),
        in_specs=[
            pl.BlockSpec((rows, d), lambda l, j: (0, 0)),
            pl.BlockSpec((None, d, tn), lambda l, j: (l, 0, j)),
            pl.BlockSpec((None, 1, tn), lambda l, j: (l, 0, j)),
        ],
        out_specs=pl.BlockSpec((None, rows, tn), lambda l, j: (l, 0, j)),
        out_shape=jax.ShapeDtypeStruct((depth, rows, n3), F32),
        compiler_params=pltpu.CompilerParams(
            dimension_semantics=("arbitrary", "arbitrary"), vmem_limit_bytes=VMEM_LIMIT),
    )(c_pad, ada_w, ada_b3)


def _proj_kernel(x_ref, mod_ref, g_ref, w_ref, wl_ref, o_ref, ol_ref, h_ref):
    @pl.when(pl.program_id(1) == 0)
    def _():
        x = x_ref[...]
        ms = jnp.mean(x * x, axis=-1, keepdims=True)
        y = x * lax.rsqrt(ms + RMS_EPS) * g_ref[0:1, :]
        h = (y * (1.0 + mod_ref[1:2, :]) + mod_ref[0:1, :]).astype(BF16)
        h_ref[...] = h
        ol_ref[...] = _dot(h, wl_ref[...]).astype(BF16)

    o_ref[...] = _dot(h_ref[...], w_ref[...]).astype(BF16)


def _proj(x2, mod4, gvec, w_in_b, layer, seq, tm, tn, d_a):
    bt, d = x2.shape
    rows_per_seq = seq // tm
    cols_a = 3 * d_a + 2 * LORA
    off_b = cols_a + d_a
    off_gate = off_b + 4 * d_a
    n_main = 12 * d_a
    assert d_a % tn == 0

    def w_col(j):
        new = j * tn
        old = jnp.where(new < 4 * d_a, new + off_gate,
                        jnp.where(new < 8 * d_a, new + (off_b - 4 * d_a),
                                  jnp.where(new < 9 * d_a, new + (cols_a - 8 * d_a),
                                            new - 9 * d_a)))
        return pl.multiple_of(old, LANES)

    return pl.pallas_call(
        _proj_kernel,
        name="in_proj",
        grid=(bt // tm, n_main // tn),
        in_specs=[
            pl.BlockSpec((tm, d), lambda i, j: (i, 0)),
            pl.BlockSpec((None, None, 3, d), lambda i, j: (layer, i // rows_per_seq, 0, 0)),
            pl.BlockSpec((None, 8, d), lambda i, j: (layer, 0, 0)),
            pl.BlockSpec((pl.Squeezed(), pl.Element(d), pl.Element(tn)),
                         lambda i, j: (layer, 0, w_col(j))),
            pl.BlockSpec((pl.Squeezed(), pl.Element(d), pl.Element(2 * LORA)),
                         lambda i, j: (layer, 0, 3 * d_a)),
        ],
        out_specs=[pl.BlockSpec((tm, tn), lambda i, j: (i, j)),
                   pl.BlockSpec((tm, 2 * LORA), lambda i, j: (i, 0))],
        out_shape=[jax.ShapeDtypeStruct((bt, n_main), BF16),
                   jax.ShapeDtypeStruct((bt, 2 * LORA), BF16)],
        scratch_shapes=[pltpu.VMEM((tm, d), BF16)],
        compiler_params=pltpu.CompilerParams(
            dimension_semantics=("parallel", "arbitrary"), vmem_limit_bytes=VMEM_LIMIT),
    )(x2, mod4, gvec, w_in_b, w_in_b)


def _wkv_kernel(r_ref, k_ref, v_ref, lx_ref, z_ref, pv_ref, wl_ref, o_ref,
                s_ref, pr_ref, pk_ref, pvp_ref, pl_ref, lora_ref,
                a_st, r_st, b_st, k_st, bd_st, kd_st, v_st, bonus_st, wmid_st, wend_st):
    step = pl.program_id(1)
    nb, C, d_a = r_ref.shape
    n_pairs = d_a // LANES
    units = [(b, hp) for b in range(nb) for hp in range(n_pairs)]
    n_units = len(units)
    slot_w = step % 2
    slot_r = 1 - slot_w

    @pl.when(step == 0)
    def _():
        s_ref[...] = jnp.zeros_like(s_ref)
        pr_ref[...] = jnp.zeros_like(pr_ref)
        pk_ref[...] = jnp.zeros_like(pk_ref)
        pvp_ref[...] = jnp.zeros_like(pvp_ref)
        pl_ref[...] = jnp.zeros_like(pl_ref)
        for ref in (a_st, r_st, b_st, k_st, bd_st, kd_st, v_st, bonus_st, wmid_st, wend_st):
            ref[1] = jnp.zeros(ref.shape[1:], ref.dtype)

    row = lax.broadcasted_iota(jnp.int32, (C, LANES), 0)
    lane = lax.broadcasted_iota(jnp.int32, (C, LANES), 1)
    col = lane & (HEAD_SIZE - 1)
    head0 = lane < HEAD_SIZE
    strict = col < row
    incl = col <= row
    same_sub = (col // SUB) == (row // SUB)
    eye = jnp.where(col == row, 1.0, 0.0).astype(F32)
    eye_c = jnp.where((lax.broadcasted_iota(jnp.int32, (SUB, LANES), 1) & (SUB - 1))
                      == lax.broadcasted_iota(jnp.int32, (SUB, LANES), 0), 1.0, 0.0).astype(F32)
    first_row = row == 0

    sq_r = lax.broadcasted_iota(jnp.int32, (LANES, LANES), 0)
    sq_c = lax.broadcasted_iota(jnp.int32, (LANES, LANES), 1)
    same_head = (sq_r < HEAD_SIZE) == (sq_c < HEAD_SIZE)
    same_sub_sq = (sq_r // SUB) == (sq_c // SUB)
    ones_bd = jnp.where(same_head, 1.0, 0.0).astype(BF16)
    mean_bd = jnp.where(same_head, 1.0 / HEAD_SIZE, 0.0).astype(BF16)
    tr = lax.broadcasted_iota(jnp.int32, (C, C), 0)
    tc = lax.broadcasted_iota(jnp.int32, (C, C), 1)
    tril = jnp.where(tc <= tr, 1.0, 0.0).astype(BF16)

    def lanes_of(hp):
        return slice(hp * LANES, (hp + 1) * LANES)

    def shifted(x, prev):
        return jnp.where(first_row, prev, pltpu.roll(x, 1, 0))

    def stack2(x):
        z = jnp.zeros_like(x)
        return jnp.concatenate([jnp.where(head0, x, z), jnp.where(head0, z, x)], axis=0)

    def mm(x, y_stacked):
        return _dot(x.astype(BF16), y_stacked)

    def seg_sum(x):
        return _dot(x.astype(BF16), ones_bd)

    def prep_lora(b):
        lx = lx_ref[b].astype(F32)
        lx_s = lx + (shifted(lx, pl_ref[b]) - lx) * pv_ref[ROW_MUL:ROW_MUL + 1, 0:LANES]
        lora_in = jnp.where(head0, jnp.tanh(lx_s), lx_s).astype(BF16)
        lora_ref[b] = _dot(lora_in, wl_ref[...])
        pl_ref[b] = lx[C - 1:C, :]

    def prep_unit(b, hp):
        sl = lanes_of(hp)
        par = lambda r_: pv_ref[r_:r_ + 1, sl]
        r_raw = r_ref[b, :, sl].astype(F32)
        k_raw = k_ref[b, :, sl].astype(F32)
        v_raw = v_ref[b, :, sl].astype(F32)
        r = r_raw + (shifted(r_raw, pr_ref[b, :, sl]) - r_raw) * par(ROW_MUR)
        k = k_raw + (shifted(k_raw, pk_ref[b, :, sl]) - k_raw) * par(ROW_MUK)
        v = v_raw + (shifted(v_raw, pvp_ref[b, :, sl]) - v_raw) * par(ROW_MUV)
        pr_ref[b, :, sl] = r_raw[C - 1:C, :]
        pk_ref[b, :, sl] = k_raw[C - 1:C, :]
        pvp_ref[b, :, sl] = v_raw[C - 1:C, :]

        logw = (-math.exp(-0.5) * LOG2_E) * _sigmoid(par(ROW_W0) + lora_ref[b, :, sl])
        lr = _sigmoid(par(ROW_A0) + lora_ref[b, :, d_a + hp * LANES:d_a + (hp + 1) * LANES])

        kk = k * par(ROW_KK)
        kk = kk * lax.rsqrt(jnp.maximum(seg_sum(kk * kk), 1e-24))
        kmod = k * (1.0 + (lr - 1.0) * par(ROW_KA))
        bvec = kk * lr
        bonus_st[slot_w, b, :, sl] = seg_sum(r * kmod * par(ROW_RK)) * v

        p_inc = _split_dot_left(tril, logw)
        p_mid = p_inc[C // 2 - 1: C // 2, :]
        p_end = p_inc[C - 1: C, :]
        pc = p_inc - p_mid
        e_inv = jnp.exp2(-pc)
        e_out = jnp.exp2(p_end - p_inc)
        a_st[slot_w, b, :, sl] = (-kk * jnp.exp2(pc - logw)).astype(BF16)
        r_st[slot_w, b, :, sl] = (r * jnp.exp2(pc)).astype(BF16)
        b_st[slot_w, b, :, sl] = (bvec * e_inv).astype(BF16)
        k_st[slot_w, b, :, sl] = (kmod * e_inv).astype(BF16)
        bd_st[slot_w, b, :, sl] = (bvec * e_out).astype(BF16)
        kd_st[slot_w, b, :, sl] = (kmod * e_out).astype(BF16)
        v_st[slot_w, b, :, sl] = v.astype(BF16)
        wmid_st[slot_w, b, :, sl] = jnp.exp2(p_mid)
        wend_st[slot_w, b, :, sl] = jnp.exp2(p_end)

    env = {}

    def ld(ref, b, hp):
        return ref[slot_r, b, :, lanes_of(hp)]

    def st_scores():
        env["ar"] = [jnp.concatenate([ld(a_st, b, hp), ld(r_st, b, hp)], axis=0) for b, hp in units]
        bk = [jnp.concatenate([stack2(ld(b_st, b, hp)), stack2(ld(k_st, b, hp))], axis=0)
              for b, hp in units]
        env["a_all"] = [_dot_nt(env["ar"][u], bk[u]) for u in range(n_units)]

    def st_state_in():
        env["s_prev"] = [s_ref[u] for u in range(n_units)]
        s_hat = [(env["s_prev"][u] * ld(wmid_st, b, hp)).astype(BF16)
                 for u, (b, hp) in enumerate(units)]
        env["ar_s"] = [_dot_nt(env["ar"][u], s_hat[u]) for u in range(n_units)]
        env["v_stk"] = [stack2(ld(v_st, b, hp)) for b, hp in units]
        a_all = env["a_all"]
        env["a_ab"] = [jnp.where(strict, a[:C, :LANES], 0.0) for a in a_all]
        a_ak = [jnp.where(strict, a[:C, LANES:], 0.0) for a in a_all]
        env["a_r"] = [jnp.concatenate([jnp.where(incl, a[C:, :LANES], 0.0),
                                       jnp.where(incl, a[C:, LANES:], 0.0)], axis=1).astype(BF16)
                      for a in a_all]
        env["rhs"] = [env["ar_s"][u][:C] + mm(a_ak[u], env["v_stk"][u]) for u in range(n_units)]

    def diag_rhs(x_c):
        return jnp.where(same_sub_sq, jnp.concatenate([x_c] * (LANES // SUB), axis=0),
                         0.0).astype(BF16)

    def fused_step(p, l_pow):
        x = mm(jnp.concatenate([p, l_pow], axis=0), diag_rhs(l_pow))
        return p + x[:SUB], x[SUB:]

    def st_t1():
        a_ab = env["a_ab"]
        l_d = [jnp.where(same_sub, a, 0.0) for a in a_ab]
        env["l_o"] = [stack2((a_ab[u] - l_d[u]).astype(BF16)) for u in range(n_units)]
        l_c = [sum(a[i * SUB:(i + 1) * SUB] for i in range(1, C // SUB)) + a[:SUB] for a in l_d]
        env["l_pow"] = [mm(l_c[u], diag_rhs(l_c[u])) for u in range(n_units)]
        env["p"] = [eye_c + a for a in l_c]

    def st_t2():
        res = [fused_step(env["p"][u], env["l_pow"][u]) for u in range(n_units)]
        env["p"] = [r_[0] for r_ in res]
        env["l_pow"] = [r_[1] for r_ in res]

    def st_t4():
        p, l8 = env["p"], env["l_pow"]
        t_c = [p[u] + mm(p[u], diag_rhs(l8[u])) for u in range(n_units)]
        env["t_d"] = [jnp.where(same_sub, jnp.concatenate([t] * (C // SUB), axis=0), 0.0)
                      for t in t_c]

    def st_n1():
        env["n1"] = [mm(env["t_d"][u], env["l_o"][u]) for u in range(n_units)]

    def st_n2():
        n1 = env["n1"]
        env["n2"] = [mm(n1[u], stack2(n1[u].astype(BF16))) for u in range(n_units)]

    def st_q():
        n1, n2 = env["n1"], env["n2"]
        env["q"] = [(eye + n1[u]) + mm(eye + n1[u], stack2(n2[u].astype(BF16)))
                    for u in range(n_units)]

    def st_tinv():
        env["t_inv"] = [mm(env["q"][u], stack2(env["t_d"][u].astype(BF16))) for u in range(n_units)]

    def st_u():
        env["u_b"] = [mm(env["t_inv"][u], stack2(env["rhs"][u].astype(BF16))).astype(BF16)
                      for u in range(n_units)]

    def st_y_state():
        u_b = env["u_b"]
        env["y"] = [env["ar_s"][u][C:] + _dot(env["a_r"][u],
                                              jnp.concatenate([stack2(u_b[u]), env["v_stk"][u]], axis=0))
                    for u in range(n_units)]
        for u, (b, hp) in enumerate(units):
            uv = jnp.concatenate([u_b[u], ld(v_st, b, hp)], axis=0)
            bk_d = jnp.concatenate([ld(bd_st, b, hp), ld(kd_st, b, hp)], axis=0)
            s_new = env["s_prev"][u] * ld(wend_st, b, hp) + _dot_tn(uv, bk_d)
            s_ref[u] = jnp.where(same_head, s_new, 0.0)

    def st_mean():
        y = jnp.concatenate(env["y"], axis=0)
        env["yc"] = y - _dot(y.astype(BF16), mean_bd)

    def st_out():
        yc = env["yc"]
        var = _dot((yc * yc).astype(BF16), mean_bd)
        yn_all = yc * lax.rsqrt(var + GN_EPS)
        for u, (b, hp) in enumerate(units):
            sl = lanes_of(hp)
            yn = yn_all[u * C:(u + 1) * C] * pv_ref[ROW_LNG:ROW_LNG + 1, sl] + pv_ref[ROW_LNB:ROW_LNB + 1, sl]
            z = z_ref[b, :, sl].astype(F32)
            o_ref[b, :, sl] = ((yn + ld(bonus_st, b, hp)) * (z * _sigmoid(z))).astype(BF16)

    stages = [st_scores, st_state_in, st_t1, st_t2, st_t2, st_t4, st_n1, st_n2, st_q, st_tinv,
              st_u, st_y_state, st_mean, st_out]

    for b in range(nb):
        prep_lora(b)
    per_stage = -(-n_units // (len(stages) - 4))
    todo = list(units)
    for stage in stages:
        stage()
        for b, hp in todo[:per_stage]:
            prep_unit(b, hp)
        todo = todo[per_stage:]
    assert not todo


def _wkv(proj3, lx3, pvec, wl, layer, d_a, nb):
    bsz, seq, _ = proj3.shape
    C = CHUNK
    n_chunks = seq // C
    n_pairs = d_a // LANES
    cur = lambda idx: (lambda b, s: (b, jnp.minimum(s, n_chunks - 1), idx))
    prev = lambda idx: (lambda b, s: (b, jnp.maximum(s - 1, 0), idx))
    stash = lambda dt: pltpu.VMEM((2, nb, C, d_a), dt)
    return pl.pallas_call(
        _wkv_kernel,
        name="wkv7_chunked",
        grid=(bsz // nb, n_chunks + 1),
        in_specs=[
            pl.BlockSpec((nb, C, d_a), cur(9)),
            pl.BlockSpec((nb, C, d_a), cur(10)),
            pl.BlockSpec((nb, C, d_a), cur(11)),
            pl.BlockSpec((nb, C, LANES), cur(0)),
            pl.BlockSpec((nb, C, d_a), prev(8)),
            pl.BlockSpec((None, N_ROWS, d_a), lambda b, s: (layer, 0, 0)),
            pl.BlockSpec((None, LANES, 2 * d_a), lambda b, s: (layer, 0, 0)),
        ],
        out_specs=pl.BlockSpec((nb, C, d_a), prev(0)),
        out_shape=jax.ShapeDtypeStruct((bsz, seq, d_a), BF16),
        scratch_shapes=[
            pltpu.VMEM((nb * n_pairs, LANES, LANES), F32),
            pltpu.VMEM((nb, 1, d_a), F32),
            pltpu.VMEM((nb, 1, d_a), F32),
            pltpu.VMEM((nb, 1, d_a), F32),
            pltpu.VMEM((nb, 1, LANES), F32),
            pltpu.VMEM((nb, C, 2 * d_a), F32),
            stash(BF16), stash(BF16), stash(BF16), stash(BF16), stash(BF16), stash(BF16),
            stash(BF16), stash(F32),
            pltpu.VMEM((2, nb, 1, d_a), F32), pltpu.VMEM((2, nb, 1, d_a), F32),
        ],
        compiler_params=pltpu.CompilerParams(
            dimension_semantics=("parallel", "arbitrary"), vmem_limit_bytes=VMEM_LIMIT),
    )(proj3, proj3, proj3, lx3, proj3, pvec, wl)


def _out_kernel(x_ref, ya_ref, ga_ref, gb_ref, bg_ref, cg_ref, hb_ref, zb_ref, cgh_ref, hbh_ref,
                mod_ref, g_ref, pv_ref, pa_ref, pb_ref, wo_ref, o_ref, *, tiles_per_seq):
    tm = x_ref.shape[0]
    i = pl.program_id(0)
    starts_seq = (i % tiles_per_seq) == 0

    u = cg_ref[...].astype(F32) * hb_ref[...].astype(F32)
    halo = cgh_ref[...].astype(F32) * hbh_ref[...].astype(F32)
    halo = jnp.where(starts_seq, 0.0, halo)
    row = lax.broadcasted_iota(jnp.int32, u.shape, 0)
    u1 = jnp.where(row == 0, halo[7:8, :], pltpu.roll(u, 1, 0))
    u2 = jnp.where(row == 0, halo[6:7, :], jnp.where(row == 1, halo[7:8, :], pltpu.roll(u, 2, 0)))
    conv = (pv_ref[ROW_CW0:ROW_CW0 + 1, :] * u2 + pv_ref[ROW_CW1:ROW_CW1 + 1, :] * u1
            + pv_ref[ROW_CW2:ROW_CW2 + 1, :] * u)
    zb = zb_ref[...].astype(F32)
    yb_in = bg_ref[...].astype(F32) * conv * (zb * _sigmoid(zb))

    y_a = _dot(ya_ref[...], pa_ref[...])
    y_b = _dot(yb_in.astype(BF16), pb_ref[...])
    m = _sigmoid(ga_ref[...].astype(F32)) * y_a + _sigmoid(gb_ref[...].astype(F32)) * y_b
    o = _dot(m.astype(BF16), wo_ref[...])
    ms = jnp.mean(o * o, axis=-1, keepdims=True)
    o_n = o * lax.rsqrt(ms + RMS_EPS) * g_ref[1:2, :]
    o_ref[...] = x_ref[...] + mod_ref[2:3, :] * o_n


def _out_stage(x2, ya2, proj, mod4, gvec, pvec, p_a, p_b, w_out, layer, seq, tm, d_a):
    bt, d = x2.shape
    tiles_per_seq = seq // tm
    hb8 = tm // 8
    const = lambda i: (layer, 0, 0)
    colblk = lambda idx: (lambda i: (i, idx))
    halo = lambda idx: (lambda i: (jnp.maximum(i * hb8 - 1, 0), idx))
    single = pl.Buffered(1)
    return pl.pallas_call(
        functools.partial(_out_kernel, tiles_per_seq=tiles_per_seq),
        name="out_stage",
        grid=(bt // tm,),
        in_specs=[
            pl.BlockSpec((tm, d), lambda i: (i, 0)),
            pl.BlockSpec((tm, d_a), lambda i: (i, 0)),
            pl.BlockSpec((tm, d), colblk(0)),
            pl.BlockSpec((tm, d), colblk(1)),
            pl.BlockSpec((tm, d_a), colblk(4)),
            pl.BlockSpec((tm, d_a), colblk(5)),
            pl.BlockSpec((tm, d_a), colblk(6)),
            pl.BlockSpec((tm, d_a), colblk(7)),
            pl.BlockSpec((8, d_a), halo(5)),
            pl.BlockSpec((8, d_a), halo(6)),
            pl.BlockSpec((None, None, 3, d), lambda i: (layer, i // tiles_per_seq, 0, 0)),
            pl.BlockSpec((None, 8, d), const),
            pl.BlockSpec((None, N_ROWS, d_a), const),
            pl.BlockSpec((None, d_a, d), const, pipeline_mode=single),
            pl.BlockSpec((None, d_a, d), const, pipeline_mode=single),
            pl.BlockSpec((None, d, d), const, pipeline_mode=single),
        ],
        out_specs=pl.BlockSpec((tm, d), lambda i: (i, 0)),
        out_shape=jax.ShapeDtypeStruct((bt, d), F32),
        compiler_params=pltpu.CompilerParams(
            dimension_semantics=("parallel",), vmem_limit_bytes=VMEM_LIMIT),
    )(x2, ya2, proj, proj, proj, proj, proj, proj, proj, proj, mod4, gvec, pvec, p_a, p_b, w_out)


def kernel(x, c, ada_w, ada_b, pre_gain, post_gain, w_in, mu_shift, w0, w2, a0, a2, k_k, k_a, r_k,
           lnx_gain, lnx_bias, conv_w, p_a, p_b, w_out):
    bsz, seq, d = x.shape
    depth = ada_w.shape[0]
    d_a = w0.shape[-1]
    assert d == 2 * d_a and d_a % LANES == 0 and seq % CHUNK == 0 and 2 * LORA == LANES
    assert w2.shape[1] == LORA and a2.shape[1] == LORA

    w_in_b = w_in.astype(BF16)
    pad_l = jnp.zeros((depth, d_a - 2 * LORA), F32)
    rows = [w0, a0, k_k, k_a, r_k.reshape(depth, d_a), lnx_gain, lnx_bias,
            mu_shift[:, :d_a], mu_shift[:, d_a:2 * d_a], mu_shift[:, 2 * d_a:3 * d_a],
            jnp.concatenate([mu_shift[:, 3 * d_a:], pad_l], axis=-1),
            conv_w[:, 0], conv_w[:, 1], conv_w[:, 2]]
    rows += [jnp.zeros((depth, d_a), F32)] * (N_ROWS - len(rows))
    pvec = jnp.stack(rows, axis=1)
    gvec = jnp.stack([pre_gain, post_gain] + [jnp.zeros_like(pre_gain)] * 6, axis=1)
    zl = jnp.zeros((depth, LORA, d_a), F32)
    wl = jnp.concatenate([jnp.concatenate([w2, zl], axis=-1),
                          jnp.concatenate([zl, a2], axis=-1)], axis=1).astype(BF16)
    p_a_b, p_b_b, w_out_b = p_a.astype(BF16), p_b.astype(BF16), w_out.astype(BF16)

    c_rows = 8 * pl.cdiv(bsz, 8)
    c_pad = jnp.zeros((c_rows, d), F32).at[:bsz].set(c)
    mod = _ada_mod(c_pad, ada_w, ada_b.reshape(depth, 1, 3 * d), tn=min(512, 3 * d))
    mod4 = mod[:, :bsz].reshape(depth, bsz, 3, d)

    tm_proj = min(1024, seq)
    tm_out = min(256, seq)
    nb = 2 if bsz % 2 == 0 else 1
    x2 = x.reshape(bsz * seq, d)
    for layer in range(depth):
        proj, lx = _proj(x2, mod4, gvec, w_in_b, layer, seq, tm_proj, min(1024, d_a), d_a)
        ya = _wkv(proj.reshape(bsz, seq, -1), lx.reshape(bsz, seq, -1), pvec, wl, layer, d_a, nb)
        x2 = _out_stage(x2, ya.reshape(bsz * seq, d_a), proj, mod4, gvec, pvec,
                        p_a_b, p_b_b, w_out_b, layer, seq, tm_out, d_a)
    return x2.reshape(bsz, seq, d)
```

```python
import functools
import math

import jax
import jax.numpy as jnp
from jax import lax
from jax.experimental import pallas as pl
from jax.experimental.pallas import tpu as pltpu

HEAD_SIZE = 64
LORA = 64
RMS_EPS = 1e-6
GN_EPS = 64e-5
CHUNK = 64
SUB = 16
LANES = 128
VMEM_LIMIT = 56 * 1024 * 1024
LOG2_E = math.log2(math.e)

F32 = jnp.float32
BF16 = jnp.bfloat16

(ROW_W0, ROW_A0, ROW_KK, ROW_KA, ROW_RK, ROW_LNG, ROW_LNB, ROW_MUR, ROW_MUK, ROW_MUV,
 ROW_MUL, ROW_CW0, ROW_CW1, ROW_CW2) = range(14)
N_ROWS = 16


def _dot(a, b):
    return jnp.dot(a, b, preferred_element_type=F32)


def _dot_nt(a, b):
    return lax.dot_general(a, b, (((1,), (1,)), ((), ())), preferred_element_type=F32)


def _dot_tn(a, b):
    return lax.dot_general(a, b, (((0,), (0,)), ((), ())), preferred_element_type=F32)


def _split_dot(x, m):
    hi = x.astype(BF16)
    lo = (x - hi.astype(F32)).astype(BF16)
    return _dot(hi, m) + _dot(lo, m)


def _split_dot_left(m, x):
    hi = x.astype(BF16)
    lo = (x - hi.astype(F32)).astype(BF16)
    return _dot(m, hi) + _dot(m, lo)


def _sigmoid(x):
    return 1.0 / (1.0 + jnp.exp(-x))


def _ada_kernel(c_ref, w_ref, b_ref, o_ref):
    c = c_ref[...]
    c_act = c * _sigmoid(c)
    w = w_ref[...]
    w_hi = w.astype(BF16)
    w_lo = (w - w_hi.astype(F32)).astype(BF16)
    o_ref[...] = _split_dot(c_act, w_hi) + _dot(c_act.astype(BF16), w_lo) + b_ref[...]


def _ada_mod(c_pad, ada_w, ada_b3, tn):
    depth, d, n3 = ada_w.shape
    rows = c_pad.shape[0]
    return pl.pallas_call(
        _ada_kernel,
        name="ada_mod",
        grid=(depth, pl.cdiv(n3, tn)),
        in_specs=[
            pl.BlockSpec((rows, d), lambda l, j: (0, 0)),
            pl.BlockSpec((None, d, tn), lambda l, j: (l, 0, j)),
            pl.BlockSpec((None, 1, tn), lambda l, j: (l, 0, j)),
        ],
        out_specs=pl.BlockSpec((None, rows, tn), lambda l, j: (l, 0, j)),
        out_shape=jax.ShapeDtypeStruct((depth, rows, n3), F32),
        compiler_params=pltpu.CompilerParams(
            dimension_semantics=("arbitrary", "arbitrary"), vmem_limit_bytes=VMEM_LIMIT),
    )(c_pad, ada_w, ada_b3)


def _proj_kernel(x_ref, mod_ref, g_ref, w_ref, wl_ref, o_ref, ol_ref, h_ref):
    @pl.when(pl.program_id(1) == 0)
    def _():
        x = x_ref[...]
        ms = jnp.mean(x * x, axis=-1, keepdims=True)
        y = x * lax.rsqrt(ms + RMS_EPS) * g_ref[0:1, :]
        h = (y * (1.0 + mod_ref[1:2, :]) + mod_ref[0:1, :]).astype(BF16)
        h_ref[...] = h
        ol_ref[...] = _dot(h, wl_ref[...].astype(BF16)).astype(BF16)

    o_ref[...] = _dot(h_ref[...], w_ref[...].astype(BF16)).astype(BF16)


def _proj(x2, mod4, gvec, w_in, layer, seq, tm, tn, d_a):
    bt, d = x2.shape
    rows_per_seq = seq // tm
    cols_a = 3 * d_a + 2 * LORA
    off_b = cols_a + d_a
    off_gate = off_b + 4 * d_a
    n_main = 12 * d_a
    assert d_a % tn == 0

    def w_col(j):
        new = j * tn
        old = jnp.where(new < 4 * d_a, new + off_gate,
                        jnp.where(new < 8 * d_a, new + (off_b - 4 * d_a),
                                  jnp.where(new < 9 * d_a, new + (cols_a - 8 * d_a),
                                            new - 9 * d_a)))
        return pl.multiple_of(old, LANES)

    return pl.pallas_call(
        _proj_kernel,
        name="in_proj",
        grid=(bt // tm, n_main // tn),
        in_specs=[
            pl.BlockSpec((tm, d), lambda i, j: (i, 0)),
            pl.BlockSpec((None, None, 3, d), lambda i, j: (layer, i // rows_per_seq, 0, 0)),
            pl.BlockSpec((None, 8, d), lambda i, j: (layer, 0, 0)),
            pl.BlockSpec((pl.Squeezed(), pl.Element(d), pl.Element(tn)),
                         lambda i, j: (layer, 0, w_col(j))),
            pl.BlockSpec((pl.Squeezed(), pl.Element(d), pl.Element(2 * LORA)),
                         lambda i, j: (layer, 0, 3 * d_a)),
        ],
        out_specs=[pl.BlockSpec((tm, tn), lambda i, j: (i, j)),
                   pl.BlockSpec((tm, 2 * LORA), lambda i, j: (i, 0))],
        out_shape=[jax.ShapeDtypeStruct((bt, n_main), BF16),
                   jax.ShapeDtypeStruct((bt, 2 * LORA), BF16)],
        scratch_shapes=[pltpu.VMEM((tm, d), BF16)],
        compiler_params=pltpu.CompilerParams(
            dimension_semantics=("parallel", "arbitrary"), vmem_limit_bytes=VMEM_LIMIT),
    )(x2, mod4, gvec, w_in, w_in)


def _wkv_kernel(r_ref, k_ref, v_ref, lx_ref, z_ref, pv_ref, wl_ref, o_ref,
                s_ref, pr_ref, pk_ref, pvp_ref, pl_ref, lora_ref,
                a_st, r_st, b_st, k_st, bd_st, kd_st, v_st, bonus_st, wmid_st, wend_st):
    step = pl.program_id(1)
    nb, C, d_a = r_ref.shape
    n_pairs = d_a // LANES
    units = [(b, hp) for b in range(nb) for hp in range(n_pairs)]
    n_units = len(units)
    slot_w = step % 2
    slot_r = 1 - slot_w

    @pl.when(step == 0)
    def _():
        s_ref[...] = jnp.zeros_like(s_ref)
        pr_ref[...] = jnp.zeros_like(pr_ref)
        pk_ref[...] = jnp.zeros_like(pk_ref)
        pvp_ref[...] = jnp.zeros_like(pvp_ref)
        pl_ref[...] = jnp.zeros_like(pl_ref)
        for ref in (a_st, r_st, b_st, k_st, bd_st, kd_st, v_st, bonus_st, wmid_st, wend_st):
            ref[1] = jnp.zeros(ref.shape[1:], ref.dtype)

    row = lax.broadcasted_iota(jnp.int32, (C, LANES), 0)
    lane = lax.broadcasted_iota(jnp.int32, (C, LANES), 1)
    col = lane & (HEAD_SIZE - 1)
    head0 = lane < HEAD_SIZE
    strict = col < row
    incl = col <= row
    same_sub = (col // SUB) == (row // SUB)
    eye = jnp.where(col == row, 1.0, 0.0).astype(F32)
    eye_c = jnp.where((lax.broadcasted_iota(jnp.int32, (SUB, LANES), 1) & (SUB - 1))
                      == lax.broadcasted_iota(jnp.int32, (SUB, LANES), 0), 1.0, 0.0).astype(F32)
    first_row = row == 0

    sq_r = lax.broadcasted_iota(jnp.int32, (LANES, LANES), 0)
    sq_c = lax.broadcasted_iota(jnp.int32, (LANES, LANES), 1)
    same_head = (sq_r < HEAD_SIZE) == (sq_c < HEAD_SIZE)
    same_sub_sq = (sq_r // SUB) == (sq_c // SUB)
    ones_bd = jnp.where(same_head, 1.0, 0.0).astype(BF16)
    mean_bd = jnp.where(same_head, 1.0 / HEAD_SIZE, 0.0).astype(BF16)
    tr = lax.broadcasted_iota(jnp.int32, (C, C), 0)
    tc = lax.broadcasted_iota(jnp.int32, (C, C), 1)
    tril = jnp.where(tc <= tr, 1.0, 0.0).astype(BF16)

    def lanes_of(hp):
        return slice(hp * LANES, (hp + 1) * LANES)

    def shifted(x, prev):
        return jnp.where(first_row, prev, pltpu.roll(x, 1, 0))

    def stack2(x):
        z = jnp.zeros_like(x)
        return jnp.concatenate([jnp.where(head0, x, z), jnp.where(head0, z, x)], axis=0)

    def mm(x, y_stacked):
        return _dot(x.astype(BF16), y_stacked)

    def seg_sum(x):
        return _dot(x.astype(BF16), ones_bd)

    def prep_lora():
        lora_in = []
        for b in range(nb):
            lx = lx_ref[b].astype(F32)
            lx_s = lx + (shifted(lx, pl_ref[b]) - lx) * pv_ref[ROW_MUL:ROW_MUL + 1, 0:LANES]
            lora_in.append(jnp.where(head0, jnp.tanh(lx_s), lx_s).astype(BF16))
            pl_ref[b] = lx[C - 1:C, :]
        lora = _dot(jnp.concatenate(lora_in, axis=0), wl_ref[...])
        for b in range(nb):
            lora_ref[b] = lora[b * C:(b + 1) * C]

    def prep_unit(b, hp):
        sl = lanes_of(hp)
        par = lambda r_: pv_ref[r_:r_ + 1, sl]
        r_raw = r_ref[b, :, sl].astype(F32)
        k_raw = k_ref[b, :, sl].astype(F32)
        v_raw = v_ref[b, :, sl].astype(F32)
        r = r_raw + (shifted(r_raw, pr_ref[b, :, sl]) - r_raw) * par(ROW_MUR)
        k = k_raw + (shifted(k_raw, pk_ref[b, :, sl]) - k_raw) * par(ROW_MUK)
        v = v_raw + (shifted(v_raw, pvp_ref[b, :, sl]) - v_raw) * par(ROW_MUV)
        pr_ref[b, :, sl] = r_raw[C - 1:C, :]
        pk_ref[b, :, sl] = k_raw[C - 1:C, :]
        pvp_ref[b, :, sl] = v_raw[C - 1:C, :]

        logw = (-math.exp(-0.5) * LOG2_E) * _sigmoid(par(ROW_W0) + lora_ref[b, :, sl])
        lr = _sigmoid(par(ROW_A0) + lora_ref[b, :, d_a + hp * LANES:d_a + (hp + 1) * LANES])

        kk = k * par(ROW_KK)
        kmod = k * (1.0 + (lr - 1.0) * par(ROW_KA))
        sums = seg_sum(jnp.concatenate([kk * kk, r * kmod * par(ROW_RK)], axis=0))
        kk = kk * lax.rsqrt(jnp.maximum(sums[:C], 1e-24))
        bvec = kk * lr
        bonus_st[slot_w, b, :, sl] = sums[C:] * v

        p_inc = _split_dot_left(tril, logw)
        p_mid = p_inc[C // 2 - 1: C // 2, :]
        p_end = p_inc[C - 1: C, :]
        pc = p_inc - p_mid
        e_inv = jnp.exp2(-pc)
        e_out = jnp.exp2(p_end - p_inc)
        a_st[slot_w, b, :, sl] = (-kk * jnp.exp2(pc - logw)).astype(BF16)
        r_st[slot_w, b, :, sl] = (r * jnp.exp2(pc)).astype(BF16)
        b_st[slot_w, b, :, sl] = (bvec * e_inv).astype(BF16)
        k_st[slot_w, b, :, sl] = (kmod * e_inv).astype(BF16)
        bd_st[slot_w, b, :, sl] = (bvec * e_out).astype(BF16)
        kd_st[slot_w, b, :, sl] = (kmod * e_out).astype(BF16)
        v_st[slot_w, b, :, sl] = v.astype(BF16)
        wmid_st[slot_w, b, :, sl] = jnp.exp2(p_mid)
        wend_st[slot_w, b, :, sl] = jnp.exp2(p_end)

    env = {}

    def ld(ref, b, hp):
        return ref[slot_r, b, :, lanes_of(hp)]

    def st_scores():
        env["ar"] = [jnp.concatenate([ld(a_st, b, hp), ld(r_st, b, hp)], axis=0) for b, hp in units]
        bk = [jnp.concatenate([stack2(ld(b_st, b, hp)), stack2(ld(k_st, b, hp))], axis=0)
              for b, hp in units]
        env["a_all"] = [_dot_nt(env["ar"][u], bk[u]) for u in range(n_units)]

    def st_state_in():
        env["s_prev"] = [s_ref[u] for u in range(n_units)]
        s_hat = [(env["s_prev"][u] * ld(wmid_st, b, hp)).astype(BF16)
                 for u, (b, hp) in enumerate(units)]
        env["ar_s"] = [_dot_nt(env["ar"][u], s_hat[u]) for u in range(n_units)]
        a_all = env["a_all"]
        env["a_ab"] = [jnp.where(strict, a[:C, :LANES], 0.0) for a in a_all]
        env["a_rb"] = [jnp.where(incl, a[C:, :LANES], 0.0) for a in a_all]
        a_k = [jnp.concatenate([jnp.where(strict, a[:C, LANES:], 0.0),
                                jnp.where(incl, a[C:, LANES:], 0.0)], axis=0) for a in a_all]
        env["a_v"] = [mm(a_k[u], stack2(ld(v_st, b, hp))) for u, (b, hp) in enumerate(units)]
        env["rhs"] = [env["ar_s"][u][:C] + env["a_v"][u][:C] for u in range(n_units)]

    def diag_rhs(x_c):
        return jnp.where(same_sub_sq, jnp.concatenate([x_c] * (LANES // SUB), axis=0),
                         0.0).astype(BF16)

    def fused_step(p, l_pow):
        x = mm(jnp.concatenate([p, l_pow], axis=0), diag_rhs(l_pow))
        return p + x[:SUB], x[SUB:]

    def st_t1():
        a_ab = env["a_ab"]
        l_d = [jnp.where(same_sub, a, 0.0) for a in a_ab]
        env["l_o"] = [stack2((a_ab[u] - l_d[u]).astype(BF16)) for u in range(n_units)]
        l_c = [sum(a[i * SUB:(i + 1) * SUB] for i in range(1, C // SUB)) + a[:SUB] for a in l_d]
        env["l_pow"] = [mm(l_c[u], diag_rhs(l_c[u])) for u in range(n_units)]
        env["p"] = [eye_c + a for a in l_c]

    def st_t2():
        res = [fused_step(env["p"][u], env["l_pow"][u]) for u in range(n_units)]
        env["p"] = [r_[0] for r_ in res]
        env["l_pow"] = [r_[1] for r_ in res]

    def st_t4():
        p, l8 = env["p"], env["l_pow"]
        t_c = [p[u] + mm(p[u], diag_rhs(l8[u])) for u in range(n_units)]
        env["t_d"] = [jnp.where(same_sub, jnp.concatenate([t] * (C // SUB), axis=0), 0.0)
                      for t in t_c]

    def st_n1():
        env["n1"] = [mm(env["t_d"][u], env["l_o"][u]) for u in range(n_units)]

    def st_n2():
        n1 = env["n1"]
        env["n2"] = [mm(n1[u], stack2(n1[u].astype(BF16))) for u in range(n_units)]

    def st_q():
        n1, n2 = env["n1"], env["n2"]
        env["q"] = [(eye + n1[u]) + mm(eye + n1[u], stack2(n2[u].astype(BF16)))
                    for u in range(n_units)]

    def st_tinv():
        env["t_inv"] = [mm(env["q"][u], stack2(env["t_d"][u].astype(BF16))) for u in range(n_units)]

    def st_u():
        env["u_b"] = [mm(env["t_inv"][u], stack2(env["rhs"][u].astype(BF16))).astype(BF16)
                      for u in range(n_units)]

    def st_y_state():
        u_b = env["u_b"]
        env["y"] = [env["ar_s"][u][C:] + env["a_v"][u][C:] + mm(env["a_rb"][u], stack2(u_b[u]))
                    for u in range(n_units)]
        for u, (b, hp) in enumerate(units):
            uv = jnp.concatenate([u_b[u], ld(v_st, b, hp)], axis=0)
            bk_d = jnp.concatenate([ld(bd_st, b, hp), ld(kd_st, b, hp)], axis=0)
            s_new = env["s_prev"][u] * ld(wend_st, b, hp) + _dot_tn(uv, bk_d)
            s_ref[u] = jnp.where(same_head, s_new, 0.0)

    def st_mean():
        y = jnp.concatenate(env["y"], axis=0)
        env["yc"] = y - _dot(y.astype(BF16), mean_bd)

    def st_out():
        yc = env["yc"]
        var = _dot((yc * yc).astype(BF16), mean_bd)
        yn_all = yc * lax.rsqrt(var + GN_EPS)
        for u, (b, hp) in enumerate(units):
            sl = lanes_of(hp)
            yn = yn_all[u * C:(u + 1) * C] * pv_ref[ROW_LNG:ROW_LNG + 1, sl] + pv_ref[ROW_LNB:ROW_LNB + 1, sl]
            z = z_ref[b, :, sl].astype(F32)
            o_ref[b, :, sl] = ((yn + ld(bonus_st, b, hp)) * (z * _sigmoid(z))).astype(BF16)

    stages = [st_scores, st_state_in, st_t1, st_t2, st_t2, st_t4, st_n1, st_n2, st_q, st_tinv,
              st_u, st_y_state, st_mean, st_out]

    prep_lora()
    per_stage = -(-n_units // (len(stages) - 4))
    todo = list(units)
    for stage in stages:
        stage()
        for b, hp in todo[:per_stage]:
            prep_unit(b, hp)
        todo = todo[per_stage:]
    assert not todo


def _wkv(proj3, lx3, pvec, wl, layer, d_a, nb):
    bsz, seq, _ = proj3.shape
    C = CHUNK
    n_chunks = seq // C
    n_pairs = d_a // LANES
    cur = lambda idx: (lambda b, s: (b, jnp.minimum(s, n_chunks - 1), idx))
    prev = lambda idx: (lambda b, s: (b, jnp.maximum(s - 1, 0), idx))
    stash = lambda dt: pltpu.VMEM((2, nb, C, d_a), dt)
    return pl.pallas_call(
        _wkv_kernel,
        name="wkv7_chunked",
        grid=(bsz // nb, n_chunks + 1),
        in_specs=[
            pl.BlockSpec((nb, C, d_a), cur(9)),
            pl.BlockSpec((nb, C, d_a), cur(10)),
            pl.BlockSpec((nb, C, d_a), cur(11)),
            pl.BlockSpec((nb, C, LANES), cur(0)),
            pl.BlockSpec((nb, C, d_a), prev(8)),
            pl.BlockSpec((None, N_ROWS, d_a), lambda b, s: (layer, 0, 0)),
            pl.BlockSpec((None, LANES, 2 * d_a), lambda b, s: (layer, 0, 0)),
        ],
        out_specs=pl.BlockSpec((nb, C, d_a), prev(0)),
        out_shape=jax.ShapeDtypeStruct((bsz, seq, d_a), BF16),
        scratch_shapes=[
            pltpu.VMEM((nb * n_pairs, LANES, LANES), F32),
            pltpu.VMEM((nb, 1, d_a), F32),
            pltpu.VMEM((nb, 1, d_a), F32),
            pltpu.VMEM((nb, 1, d_a), F32),
            pltpu.VMEM((nb, 1, LANES), F32),
            pltpu.VMEM((nb, C, 2 * d_a), F32),
            stash(BF16), stash(BF16), stash(BF16), stash(BF16), stash(BF16), stash(BF16),
            stash(BF16), stash(F32),
            pltpu.VMEM((2, nb, 1, d_a), F32), pltpu.VMEM((2, nb, 1, d_a), F32),
        ],
        compiler_params=pltpu.CompilerParams(
            dimension_semantics=("parallel", "arbitrary"), vmem_limit_bytes=VMEM_LIMIT),
    )(proj3, proj3, proj3, lx3, proj3, pvec, wl)


def _out_kernel(x_ref, ya_ref, ga_ref, gb_ref, bg_ref, cg_ref, hb_ref, zb_ref, cgh_ref, hbh_ref,
                mod_ref, g_ref, pv_ref, pa_ref, pb_ref, wo_ref, o_ref, *, tiles_per_seq):
    tm = x_ref.shape[0]
    i = pl.program_id(0)
    starts_seq = (i % tiles_per_seq) == 0

    u = cg_ref[...].astype(F32) * hb_ref[...].astype(F32)
    halo = cgh_ref[...].astype(F32) * hbh_ref[...].astype(F32)
    halo = jnp.where(starts_seq, 0.0, halo)
    row = lax.broadcasted_iota(jnp.int32, u.shape, 0)
    u1 = jnp.where(row == 0, halo[7:8, :], pltpu.roll(u, 1, 0))
    u2 = jnp.where(row == 0, halo[6:7, :], jnp.where(row == 1, halo[7:8, :], pltpu.roll(u, 2, 0)))
    conv = (pv_ref[ROW_CW0:ROW_CW0 + 1, :] * u2 + pv_ref[ROW_CW1:ROW_CW1 + 1, :] * u1
            + pv_ref[ROW_CW2:ROW_CW2 + 1, :] * u)
    zb = zb_ref[...].astype(F32)
    yb_in = bg_ref[...].astype(F32) * conv * (zb * _sigmoid(zb))

    y_a = _dot(ya_ref[...], pa_ref[...])
    y_b = _dot(yb_in.astype(BF16), pb_ref[...])
    m = _sigmoid(ga_ref[...].astype(F32)) * y_a + _sigmoid(gb_ref[...].astype(F32)) * y_b
    o = _dot(m.astype(BF16), wo_ref[...])
    ms = jnp.mean(o * o, axis=-1, keepdims=True)
    o_n = o * lax.rsqrt(ms + RMS_EPS) * g_ref[1:2, :]
    o_ref[...] = x_ref[...] + mod_ref[2:3, :] * o_n


def _out_stage(x2, ya2, proj, mod4, gvec, pvec, p_a, p_b, w_out, layer, seq, tm, d_a):
    bt, d = x2.shape
    tiles_per_seq = seq // tm
    hb8 = tm // 8
    const = lambda i: (layer, 0, 0)
    colblk = lambda idx: (lambda i: (i, idx))
    halo = lambda idx: (lambda i: (jnp.maximum(i * hb8 - 1, 0), idx))
    single = pl.Buffered(1)
    return pl.pallas_call(
        functools.partial(_out_kernel, tiles_per_seq=tiles_per_seq),
        name="out_stage",
        grid=(bt // tm,),
        in_specs=[
            pl.BlockSpec((tm, d), lambda i: (i, 0)),
            pl.BlockSpec((tm, d_a), lambda i: (i, 0)),
            pl.BlockSpec((tm, d), colblk(0)),
            pl.BlockSpec((tm, d), colblk(1)),
            pl.BlockSpec((tm, d_a), colblk(4)),
            pl.BlockSpec((tm, d_a), colblk(5)),
            pl.BlockSpec((tm, d_a), colblk(6)),
            pl.BlockSpec((tm, d_a), colblk(7)),
            pl.BlockSpec((8, d_a), halo(5)),
            pl.BlockSpec((8, d_a), halo(6)),
            pl.BlockSpec((None, None, 3, d), lambda i: (layer, i // tiles_per_seq, 0, 0)),
            pl.BlockSpec((None, 8, d), const),
            pl.BlockSpec((None, N_ROWS, d_a), const),
            pl.BlockSpec((None, d_a, d), const, pipeline_mode=single),
            pl.BlockSpec((None, d_a, d), const, pipeline_mode=single),
            pl.BlockSpec((None, d, d), const, pipeline_mode=single),
        ],
        out_specs=pl.BlockSpec((tm, d), lambda i: (i, 0)),
        out_shape=jax.ShapeDtypeStruct((bt, d), F32),
        compiler_params=pltpu.CompilerParams(
            dimension_semantics=("parallel",), vmem_limit_bytes=VMEM_LIMIT),
    )(x2, ya2, proj, proj, proj, proj, proj, proj, proj, proj, mod4, gvec, pvec, p_a, p_b, w_out)


def kernel(x, c, ada_w, ada_b, pre_gain, post_gain, w_in, mu_shift, w0, w2, a0, a2, k_k, k_a, r_k,
           lnx_gain, lnx_bias, conv_w, p_a, p_b, w_out):
    bsz, seq, d = x.shape
    depth = ada_w.shape[0]
    d_a = w0.shape[-1]
    assert d == 2 * d_a and d_a % LANES == 0 and seq % CHUNK == 0 and 2 * LORA == LANES
    assert w2.shape[1] == LORA and a2.shape[1] == LORA

    pad_l = jnp.zeros((depth, d_a - 2 * LORA), F32)
    rows = [w0, a0, k_k, k_a, r_k.reshape(depth, d_a), lnx_gain, lnx_bias,
            mu_shift[:, :d_a], mu_shift[:, d_a:2 * d_a], mu_shift[:, 2 * d_a:3 * d_a],
            jnp.concatenate([mu_shift[:, 3 * d_a:], pad_l], axis=-1),
            conv_w[:, 0], conv_w[:, 1], conv_w[:, 2]]
    rows += [jnp.zeros((depth, d_a), F32)] * (N_ROWS - len(rows))
    pvec = jnp.stack(rows, axis=1)
    gvec = jnp.stack([pre_gain, post_gain] + [jnp.zeros_like(pre_gain)] * 6, axis=1)
    zl = jnp.zeros((depth, LORA, d_a), F32)
    wl = jnp.concatenate([jnp.concatenate([w2, zl], axis=-1),
                          jnp.concatenate([zl, a2], axis=-1)], axis=1).astype(BF16)
    p_a_b, p_b_b, w_out_b = p_a.astype(BF16), p_b.astype(BF16), w_out.astype(BF16)

    c_rows = 8 * pl.cdiv(bsz, 8)
    c_pad = jnp.zeros((c_rows, d), F32).at[:bsz].set(c)
    mod = _ada_mod(c_pad, ada_w, ada_b.reshape(depth, 1, 3 * d), tn=min(1024, 3 * d))
    mod4 = mod[:, :bsz].reshape(depth, bsz, 3, d)

    tm_proj = min(1024, seq)
    tm_out = min(256, seq)
    nb = 2 if bsz % 2 == 0 else 1
    x2 = x.reshape(bsz * seq, d)
    for layer in range(depth):
        proj, lx = _proj(x2, mod4, gvec, w_in, layer, seq, tm_proj, min(1024, d_a), d_a)
        ya = _wkv(proj.reshape(bsz, seq, -1), lx.reshape(bsz, seq, -1), pvec, wl, layer, d_a, nb)
        x2 = _out_stage(x2, ya.reshape(bsz * seq, d_a), proj, mod4, gvec, pvec,
                        p_a_b, p_b_b, w_out_b, layer, seq, tm_out, d_a)
    return x2.reshape(bsz, seq, d)
```

```python
import functools
import math

import jax
import jax.numpy as jnp
from jax import lax
from jax.experimental import pallas as pl
from jax.experimental.pallas import tpu as pltpu

HEAD_SIZE = 64
LORA = 64
RMS_EPS = 1e-6
GN_EPS = 64e-5
CHUNK = 64
SUB = 16
LANES = 128
VMEM_LIMIT = 56 * 1024 * 1024
LOG2_E = math.log2(math.e)

F32 = jnp.float32
BF16 = jnp.bfloat16

(ROW_W0, ROW_A0, ROW_KK, ROW_KA, ROW_RK, ROW_LNG, ROW_LNB, ROW_MUR, ROW_MUK, ROW_MUV,
 ROW_MUL, ROW_CW0, ROW_CW1, ROW_CW2) = range(14)
N_ROWS = 16


def _dot(a, b):
    return jnp.dot(a, b, preferred_element_type=F32)


def _dot_nt(a, b):
    return lax.dot_general(a, b, (((1,), (1,)), ((), ())), preferred_element_type=F32)


def _dot_tn(a, b):
    return lax.dot_general(a, b, (((0,), (0,)), ((), ())), preferred_element_type=F32)


def _split_dot(x, m):
    hi = x.astype(BF16)
    lo = (x - hi.astype(F32)).astype(BF16)
    return _dot(hi, m) + _dot(lo, m)


def _split_dot_left(m, x):
    hi = x.astype(BF16)
    lo = (x - hi.astype(F32)).astype(BF16)
    return _dot(m, hi) + _dot(m, lo)


def _sigmoid(x):
    return jax.nn.sigmoid(x)


def _ada_kernel(c_ref, w_ref, b_ref, o_ref):
    c = c_ref[...]
    c_act = c * _sigmoid(c)
    w = w_ref[...]
    w_hi = w.astype(BF16)
    w_lo = (w - w_hi.astype(F32)).astype(BF16)
    o_ref[...] = _split_dot(c_act, w_hi) + _dot(c_act.astype(BF16), w_lo) + b_ref[...]


def _ada_mod(c_pad, ada_w, ada_b3, tn):
    depth, d, n3 = ada_w.shape
    rows = c_pad.shape[0]
    return pl.pallas_call(
        _ada_kernel,
        name="ada_mod",
        grid=(depth, pl.cdiv(n3, tn)),
        in_specs=[
            pl.BlockSpec((rows, d), lambda l, j: (0, 0)),
            pl.BlockSpec((None, d, tn), lambda l, j: (l, 0, j)),
            pl.BlockSpec((None, 1, tn), lambda l, j: (l, 0, j)),
        ],
        out_specs=pl.BlockSpec((None, rows, tn), lambda l, j: (l, 0, j)),
        out_shape=jax.ShapeDtypeStruct((depth, rows, n3), F32),
        compiler_params=pltpu.CompilerParams(
            dimension_semantics=("arbitrary", "arbitrary"), vmem_limit_bytes=VMEM_LIMIT),
    )(c_pad, ada_w, ada_b3)


def _proj_kernel(x_ref, mod_ref, g_ref, w_ref, wl_ref, o_ref, ol_ref, h_ref):
    @pl.when(pl.program_id(1) == 0)
    def _():
        x = x_ref[...]
        ms = jnp.mean(x * x, axis=-1, keepdims=True)
        y = x * lax.rsqrt(ms + RMS_EPS) * g_ref[0:1, :]
        h = (y * (1.0 + mod_ref[1:2, :]) + mod_ref[0:1, :]).astype(BF16)
        h_ref[...] = h
        ol_ref[...] = _dot(h, wl_ref[...].astype(BF16)).astype(BF16)

    o_ref[...] = _dot(h_ref[...], w_ref[...].astype(BF16)).astype(BF16)


def _proj(x2, mod4, gvec, w_in, layer, seq, tm, tn, d_a):
    bt, d = x2.shape
    rows_per_seq = seq // tm
    cols_a = 3 * d_a + 2 * LORA
    off_b = cols_a + d_a
    off_gate = off_b + 4 * d_a
    n_main = 12 * d_a
    assert d_a % tn == 0

    def w_col(j):
        new = j * tn
        old = jnp.where(new < 4 * d_a, new + off_gate,
                        jnp.where(new < 8 * d_a, new + (off_b - 4 * d_a),
                                  jnp.where(new < 9 * d_a, new + (cols_a - 8 * d_a),
                                            new - 9 * d_a)))
        return pl.multiple_of(old, LANES)

    return pl.pallas_call(
        _proj_kernel,
        name="in_proj",
        grid=(bt // tm, n_main // tn),
        in_specs=[
            pl.BlockSpec((tm, d), lambda i, j: (i, 0)),
            pl.BlockSpec((None, None, 3, d), lambda i, j: (layer, i // rows_per_seq, 0, 0)),
            pl.BlockSpec((None, 8, d), lambda i, j: (layer, 0, 0)),
            pl.BlockSpec((pl.Squeezed(), pl.Element(d), pl.Element(tn)),
                         lambda i, j: (layer, 0, w_col(j))),
            pl.BlockSpec((pl.Squeezed(), pl.Element(d), pl.Element(2 * LORA)),
                         lambda i, j: (layer, 0, 3 * d_a)),
        ],
        out_specs=[pl.BlockSpec((tm, tn), lambda i, j: (i, j)),
                   pl.BlockSpec((tm, 2 * LORA), lambda i, j: (i, 0))],
        out_shape=[jax.ShapeDtypeStruct((bt, n_main), BF16),
                   jax.ShapeDtypeStruct((bt, 2 * LORA), BF16)],
        scratch_shapes=[pltpu.VMEM((tm, d), BF16)],
        compiler_params=pltpu.CompilerParams(
            dimension_semantics=("parallel", "arbitrary"), vmem_limit_bytes=VMEM_LIMIT),
    )(x2, mod4, gvec, w_in, w_in)


def _wkv_kernel(r_ref, k_ref, v_ref, lx_ref, z_ref, pv_ref, wl_ref, o_ref,
                s_ref, pr_ref, pk_ref, pvp_ref, pl_ref, lora_ref,
                a_st, r_st, b_st, k_st, bd_st, kd_st, v_st, bonus_st, wmid_st, wend_st):
    step = pl.program_id(1)
    nb, C, d_a = r_ref.shape
    n_pairs = d_a // LANES
    units = [(b, hp) for b in range(nb) for hp in range(n_pairs)]
    n_units = len(units)
    slot_w = step % 2
    slot_r = 1 - slot_w

    @pl.when(step == 0)
    def _():
        s_ref[...] = jnp.zeros_like(s_ref)
        pr_ref[...] = jnp.zeros_like(pr_ref)
        pk_ref[...] = jnp.zeros_like(pk_ref)
        pvp_ref[...] = jnp.zeros_like(pvp_ref)
        pl_ref[...] = jnp.zeros_like(pl_ref)
        for ref in (a_st, r_st, b_st, k_st, bd_st, kd_st, v_st, bonus_st, wmid_st, wend_st):
            ref[1] = jnp.zeros(ref.shape[1:], ref.dtype)

    row = lax.broadcasted_iota(jnp.int32, (C, LANES), 0)
    lane = lax.broadcasted_iota(jnp.int32, (C, LANES), 1)
    col = lane & (HEAD_SIZE - 1)
    head0 = lane < HEAD_SIZE
    strict = col < row
    incl = col <= row
    same_sub = (col // SUB) == (row // SUB)
    eye = jnp.where(col == row, 1.0, 0.0).astype(F32)
    eye_c = jnp.where((lax.broadcasted_iota(jnp.int32, (SUB, LANES), 1) & (SUB - 1))
                      == lax.broadcasted_iota(jnp.int32, (SUB, LANES), 0), 1.0, 0.0).astype(F32)
    first_row = row == 0

    sq_r = lax.broadcasted_iota(jnp.int32, (LANES, LANES), 0)
    sq_c = lax.broadcasted_iota(jnp.int32, (LANES, LANES), 1)
    same_head = (sq_r < HEAD_SIZE) == (sq_c < HEAD_SIZE)
    same_sub_sq = (sq_r // SUB) == (sq_c // SUB)
    ones_bd = jnp.where(same_head, 1.0, 0.0).astype(BF16)
    mean_bd = jnp.where(same_head, 1.0 / HEAD_SIZE, 0.0).astype(BF16)
    tr = lax.broadcasted_iota(jnp.int32, (C, C), 0)
    tc = lax.broadcasted_iota(jnp.int32, (C, C), 1)
    tril = jnp.where(tc <= tr, 1.0, 0.0).astype(BF16)

    def lanes_of(hp):
        return slice(hp * LANES, (hp + 1) * LANES)

    def shifted(x, prev):
        return jnp.where(first_row, prev, pltpu.roll(x, 1, 0))

    def stack2(x):
        z = jnp.zeros_like(x)
        return jnp.concatenate([jnp.where(head0, x, z), jnp.where(head0, z, x)], axis=0)

    def mm(x, y_stacked):
        return _dot(x.astype(BF16), y_stacked)

    def seg_sum(x):
        return _dot(x.astype(BF16), ones_bd)

    def prep_lora():
        lora_in = []
        for b in range(nb):
            lx = lx_ref[b].astype(F32)
            lx_s = lx + (shifted(lx, pl_ref[b]) - lx) * pv_ref[ROW_MUL:ROW_MUL + 1, 0:LANES]
            lora_in.append(jnp.where(head0, jnp.tanh(lx_s), lx_s).astype(BF16))
            pl_ref[b] = lx[C - 1:C, :]
        lora = _dot(jnp.concatenate(lora_in, axis=0), wl_ref[...])
        for b in range(nb):
            lora_ref[b] = lora[b * C:(b + 1) * C]

    def prep_unit(b, hp):
        sl = lanes_of(hp)
        par = lambda r_: pv_ref[r_:r_ + 1, sl]
        r_raw = r_ref[b, :, sl].astype(F32)
        k_raw = k_ref[b, :, sl].astype(F32)
        v_raw = v_ref[b, :, sl].astype(F32)
        r = r_raw + (shifted(r_raw, pr_ref[b, :, sl]) - r_raw) * par(ROW_MUR)
        k = k_raw + (shifted(k_raw, pk_ref[b, :, sl]) - k_raw) * par(ROW_MUK)
        v = v_raw + (shifted(v_raw, pvp_ref[b, :, sl]) - v_raw) * par(ROW_MUV)
        pr_ref[b, :, sl] = r_raw[C - 1:C, :]
        pk_ref[b, :, sl] = k_raw[C - 1:C, :]
        pvp_ref[b, :, sl] = v_raw[C - 1:C, :]

        logw = (-math.exp(-0.5) * LOG2_E) * _sigmoid(par(ROW_W0) + lora_ref[b, :, sl])
        lr = _sigmoid(par(ROW_A0) + lora_ref[b, :, d_a + hp * LANES:d_a + (hp + 1) * LANES])

        kk = k * par(ROW_KK)
        kmod = k * (1.0 + (lr - 1.0) * par(ROW_KA))
        sums = seg_sum(jnp.concatenate([kk * kk, r * kmod * par(ROW_RK)], axis=0))
        kk = kk * lax.rsqrt(jnp.maximum(sums[:C], 1e-24))
        bvec = kk * lr
        bonus_st[slot_w, b, :, sl] = sums[C:] * v

        p_inc = _split_dot_left(tril, logw)
        p_mid = p_inc[C // 2 - 1: C // 2, :]
        p_end = p_inc[C - 1: C, :]
        pc = p_inc - p_mid
        e_inv = jnp.exp2(-pc)
        e_out = jnp.exp2(p_end - p_inc)
        a_st[slot_w, b, :, sl] = (-kk * jnp.exp2(pc - logw)).astype(BF16)
        r_st[slot_w, b, :, sl] = (r * jnp.exp2(pc)).astype(BF16)
        b_st[slot_w, b, :, sl] = (bvec * e_inv).astype(BF16)
        k_st[slot_w, b, :, sl] = (kmod * e_inv).astype(BF16)
        bd_st[slot_w, b, :, sl] = (bvec * e_out).astype(BF16)
        kd_st[slot_w, b, :, sl] = (kmod * e_out).astype(BF16)
        v_st[slot_w, b, :, sl] = v.astype(BF16)
        wmid_st[slot_w, b, :, sl] = jnp.exp2(p_mid)
        wend_st[slot_w, b, :, sl] = jnp.exp2(p_end)

    env = {}

    def ld(ref, b, hp):
        return ref[slot_r, b, :, lanes_of(hp)]

    def st_scores():
        env["ar"] = [jnp.concatenate([ld(a_st, b, hp), ld(r_st, b, hp)], axis=0) for b, hp in units]
        bk = [jnp.concatenate([stack2(ld(b_st, b, hp)), stack2(ld(k_st, b, hp))], axis=0)
              for b, hp in units]
        env["a_all"] = [_dot_nt(env["ar"][u], bk[u]) for u in range(n_units)]

    def st_state_in():
        env["s_prev"] = [s_ref[u] for u in range(n_units)]
        s_hat = [(env["s_prev"][u] * ld(wmid_st, b, hp)).astype(BF16)
                 for u, (b, hp) in enumerate(units)]
        env["ar_s"] = [_dot_nt(env["ar"][u], s_hat[u]) for u in range(n_units)]
        a_all = env["a_all"]
        env["a_ab"] = [jnp.where(strict, a[:C, :LANES], 0.0) for a in a_all]
        env["a_rb"] = [jnp.where(incl, a[C:, :LANES], 0.0) for a in a_all]
        a_k = [jnp.concatenate([jnp.where(strict, a[:C, LANES:], 0.0),
                                jnp.where(incl, a[C:, LANES:], 0.0)], axis=0) for a in a_all]
        env["a_v"] = [mm(a_k[u], stack2(ld(v_st, b, hp))) for u, (b, hp) in enumerate(units)]
        env["rhs"] = [env["ar_s"][u][:C] + env["a_v"][u][:C] for u in range(n_units)]

    def diag_rhs(x_c):
        return jnp.where(same_sub_sq, jnp.concatenate([x_c] * (LANES // SUB), axis=0),
                         0.0).astype(BF16)

    def fused_step(p, l_pow):
        x = mm(jnp.concatenate([p, l_pow], axis=0), diag_rhs(l_pow))
        return p + x[:SUB], x[SUB:]

    def st_t1():
        a_ab = env["a_ab"]
        l_d = [jnp.where(same_sub, a, 0.0) for a in a_ab]
        env["l_o"] = [stack2((a_ab[u] - l_d[u]).astype(BF16)) for u in range(n_units)]
        l_c = [sum(a[i * SUB:(i + 1) * SUB] for i in range(1, C // SUB)) + a[:SUB] for a in l_d]
        env["l_pow"] = [mm(l_c[u], diag_rhs(l_c[u])) for u in range(n_units)]
        env["p"] = [eye_c + a for a in l_c]

    def st_t2():
        res = [fused_step(env["p"][u], env["l_pow"][u]) for u in range(n_units)]
        env["p"] = [r_[0] for r_ in res]
        env["l_pow"] = [r_[1] for r_ in res]

    def st_t4():
        p, l8 = env["p"], env["l_pow"]
        t_c = [p[u] + mm(p[u], diag_rhs(l8[u])) for u in range(n_units)]
        env["t_d"] = [jnp.where(same_sub, jnp.concatenate([t] * (C // SUB), axis=0), 0.0)
                      for t in t_c]

    def st_n1():
        env["n1"] = [mm(env["t_d"][u], env["l_o"][u]) for u in range(n_units)]

    def st_n2():
        n1 = env["n1"]
        env["n2"] = [mm(n1[u], stack2(n1[u].astype(BF16))) for u in range(n_units)]

    def st_q():
        n1, n2 = env["n1"], env["n2"]
        env["q"] = [(eye + n1[u]) + mm(eye + n1[u], stack2(n2[u].astype(BF16)))
                    for u in range(n_units)]

    def st_tinv():
        env["t_inv"] = [mm(env["q"][u], stack2(env["t_d"][u].astype(BF16))) for u in range(n_units)]

    def st_u():
        env["u_b"] = [mm(env["t_inv"][u], stack2(env["rhs"][u].astype(BF16))).astype(BF16)
                      for u in range(n_units)]

    def st_y_state():
        u_b = env["u_b"]
        env["y"] = [env["ar_s"][u][C:] + env["a_v"][u][C:] + mm(env["a_rb"][u], stack2(u_b[u]))
                    for u in range(n_units)]
        for u, (b, hp) in enumerate(units):
            uv = jnp.concatenate([u_b[u], ld(v_st, b, hp)], axis=0)
            bk_d = jnp.concatenate([ld(bd_st, b, hp), ld(kd_st, b, hp)], axis=0)
            s_new = env["s_prev"][u] * ld(wend_st, b, hp) + _dot_tn(uv, bk_d)
            s_ref[u] = jnp.where(same_head, s_new, 0.0)

    def st_mean():
        y = jnp.concatenate(env["y"], axis=0)
        env["yc"] = y - _dot(y.astype(BF16), mean_bd)

    def st_out():
        yc = env["yc"]
        var = _dot((yc * yc).astype(BF16), mean_bd)
        yn_all = yc * lax.rsqrt(var + GN_EPS)
        for u, (b, hp) in enumerate(units):
            sl = lanes_of(hp)
            yn = yn_all[u * C:(u + 1) * C] * pv_ref[ROW_LNG:ROW_LNG + 1, sl] + pv_ref[ROW_LNB:ROW_LNB + 1, sl]
            z = z_ref[b, :, sl].astype(F32)
            o_ref[b, :, sl] = ((yn + ld(bonus_st, b, hp)) * (z * _sigmoid(z))).astype(BF16)

    stages = [st_scores, st_state_in, st_t1, st_t2, st_t2, st_t4, st_n1, st_n2, st_q, st_tinv,
              st_u, st_y_state, st_mean, st_out]

    prep_lora()
    per_stage = -(-n_units // (len(stages) - 4))
    todo = list(units)
    for stage in stages:
        stage()
        for b, hp in todo[:per_stage]:
            prep_unit(b, hp)
        todo = todo[per_stage:]
    assert not todo


def _wkv(proj3, lx3, pvec, wl, layer, d_a, nb):
    bsz, seq, _ = proj3.shape
    C = CHUNK
    n_chunks = seq // C
    n_pairs = d_a // LANES
    cur = lambda idx: (lambda b, s: (b, jnp.minimum(s, n_chunks - 1), idx))
    prev = lambda idx: (lambda b, s: (b, jnp.maximum(s - 1, 0), idx))
    stash = lambda dt: pltpu.VMEM((2, nb, C, d_a), dt)
    return pl.pallas_call(
        _wkv_kernel,
        name="wkv7_chunked",
        grid=(bsz // nb, n_chunks + 1),
        in_specs=[
            pl.BlockSpec((nb, C, d_a), cur(9)),
            pl.BlockSpec((nb, C, d_a), cur(10)),
            pl.BlockSpec((nb, C, d_a), cur(11)),
            pl.BlockSpec((nb, C, LANES), cur(0)),
            pl.BlockSpec((nb, C, d_a), prev(8)),
            pl.BlockSpec((None, N_ROWS, d_a), lambda b, s: (layer, 0, 0)),
            pl.BlockSpec((None, LANES, 2 * d_a), lambda b, s: (layer, 0, 0)),
        ],
        out_specs=pl.BlockSpec((nb, C, d_a), prev(0)),
        out_shape=jax.ShapeDtypeStruct((bsz, seq, d_a), BF16),
        scratch_shapes=[
            pltpu.VMEM((nb * n_pairs, LANES, LANES), F32),
            pltpu.VMEM((nb, 1, d_a), F32),
            pltpu.VMEM((nb, 1, d_a), F32),
            pltpu.VMEM((nb, 1, d_a), F32),
            pltpu.VMEM((nb, 1, LANES), F32),
            pltpu.VMEM((nb, C, 2 * d_a), F32),
            stash(BF16), stash(BF16), stash(BF16), stash(BF16), stash(BF16), stash(BF16),
            stash(BF16), stash(F32),
            pltpu.VMEM((2, nb, 1, d_a), F32), pltpu.VMEM((2, nb, 1, d_a), F32),
        ],
        compiler_params=pltpu.CompilerParams(
            dimension_semantics=("parallel", "arbitrary"), vmem_limit_bytes=VMEM_LIMIT),
    )(proj3, proj3, proj3, lx3, proj3, pvec, wl)


def _out_kernel(x_ref, ya_ref, ga_ref, gb_ref, bg_ref, cg_ref, hb_ref, zb_ref, cgh_ref, hbh_ref,
                mod_ref, g_ref, pv_ref, pa_ref, pb_ref, wo_ref, o_ref, *, tiles_per_seq):
    tm = x_ref.shape[0]
    i = pl.program_id(0)
    starts_seq = (i % tiles_per_seq) == 0

    u = cg_ref[...].astype(F32) * hb_ref[...].astype(F32)
    halo = cgh_ref[...].astype(F32) * hbh_ref[...].astype(F32)
    halo = jnp.where(starts_seq, 0.0, halo)
    row = lax.broadcasted_iota(jnp.int32, u.shape, 0)
    u1 = jnp.where(row == 0, halo[7:8, :], pltpu.roll(u, 1, 0))
    u2 = jnp.where(row == 0, halo[6:7, :], jnp.where(row == 1, halo[7:8, :], pltpu.roll(u, 2, 0)))
    conv = (pv_ref[ROW_CW0:ROW_CW0 + 1, :] * u2 + pv_ref[ROW_CW1:ROW_CW1 + 1, :] * u1
            + pv_ref[ROW_CW2:ROW_CW2 + 1, :] * u)
    zb = zb_ref[...].astype(F32)
    yb_in = bg_ref[...].astype(F32) * conv * (zb * _sigmoid(zb))

    y_a = _dot(ya_ref[...], pa_ref[...])
    y_b = _dot(yb_in.astype(BF16), pb_ref[...])
    m = _sigmoid(ga_ref[...].astype(F32)) * y_a + _sigmoid(gb_ref[...].astype(F32)) * y_b
    o = _dot(m.astype(BF16), wo_ref[...])
    ms = jnp.mean(o * o, axis=-1, keepdims=True)
    o_n = o * lax.rsqrt(ms + RMS_EPS) * g_ref[1:2, :]
    o_ref[...] = x_ref[...] + mod_ref[2:3, :] * o_n


def _out_stage(x2, ya2, proj, mod4, gvec, pvec, p_a, p_b, w_out, layer, seq, tm, d_a):
    bt, d = x2.shape
    tiles_per_seq = seq // tm
    hb8 = tm // 8
    const = lambda i: (layer, 0, 0)
    colblk = lambda idx: (lambda i: (i, idx))
    halo = lambda idx: (lambda i: (jnp.maximum(i * hb8 - 1, 0), idx))
    single = pl.Buffered(1)
    return pl.pallas_call(
        functools.partial(_out_kernel, tiles_per_seq=tiles_per_seq),
        name="out_stage",
        grid=(bt // tm,),
        in_specs=[
            pl.BlockSpec((tm, d), lambda i: (i, 0)),
            pl.BlockSpec((tm, d_a), lambda i: (i, 0)),
            pl.BlockSpec((tm, d), colblk(0)),
            pl.BlockSpec((tm, d), colblk(1)),
            pl.BlockSpec((tm, d_a), colblk(4)),
            pl.BlockSpec((tm, d_a), colblk(5)),
            pl.BlockSpec((tm, d_a), colblk(6)),
            pl.BlockSpec((tm, d_a), colblk(7)),
            pl.BlockSpec((8, d_a), halo(5)),
            pl.BlockSpec((8, d_a), halo(6)),
            pl.BlockSpec((None, None, 3, d), lambda i: (layer, i // tiles_per_seq, 0, 0)),
            pl.BlockSpec((None, 8, d), const),
            pl.BlockSpec((None, N_ROWS, d_a), const),
            pl.BlockSpec((None, d_a, d), const, pipeline_mode=single),
            pl.BlockSpec((None, d_a, d), const, pipeline_mode=single),
            pl.BlockSpec((None, d, d), const, pipeline_mode=single),
        ],
        out_specs=pl.BlockSpec((tm, d), lambda i: (i, 0)),
        out_shape=jax.ShapeDtypeStruct((bt, d), F32),
        compiler_params=pltpu.CompilerParams(
            dimension_semantics=("parallel",), vmem_limit_bytes=VMEM_LIMIT),
    )(x2, ya2, proj, proj, proj, proj, proj, proj, proj, proj, mod4, gvec, pvec, p_a, p_b, w_out)


def _tiles(bsz, seq, d, d_a):
    f32, bf16 = 4, 2
    proj_tm = min(1024, seq)
    proj_tn = min(1024, d_a)
    proj_bytes = (2 * proj_tm * d * f32 + proj_tm * d * bf16
                  + 2 * d * proj_tn * f32 + d * proj_tn * bf16
                  + 2 * proj_tm * proj_tn * bf16)
    out_tm = min(256, seq)
    out_bytes = ((d_a * d * 2 + d * d) * bf16
                 + 2 * out_tm * (2 * d * f32 + (5 * d_a + 2 * d) * bf16)
                 + 5 * out_tm * d * f32)
    assert max(proj_bytes, out_bytes) < VMEM_LIMIT
    wkv_rows = 4 if bsz % 4 == 0 else (2 if bsz % 2 == 0 else 1)
    return dict(ada_tn=min(1024, 3 * d), proj_tm=proj_tm, proj_tn=proj_tn, out_tm=out_tm,
                wkv_rows=wkv_rows)


def kernel(x, c, ada_w, ada_b, pre_gain, post_gain, w_in, mu_shift, w0, w2, a0, a2, k_k, k_a, r_k,
           lnx_gain, lnx_bias, conv_w, p_a, p_b, w_out):
    bsz, seq, d = x.shape
    depth = ada_w.shape[0]
    d_a = w0.shape[-1]
    assert d == 2 * d_a and d_a % LANES == 0 and seq % CHUNK == 0 and 2 * LORA == LANES
    assert w2.shape[1] == LORA and a2.shape[1] == LORA

    pad_l = jnp.zeros((depth, d_a - 2 * LORA), F32)
    rows = [w0, a0, k_k, k_a, r_k.reshape(depth, d_a), lnx_gain, lnx_bias,
            mu_shift[:, :d_a], mu_shift[:, d_a:2 * d_a], mu_shift[:, 2 * d_a:3 * d_a],
            jnp.concatenate([mu_shift[:, 3 * d_a:], pad_l], axis=-1),
            conv_w[:, 0], conv_w[:, 1], conv_w[:, 2]]
    rows += [jnp.zeros((depth, d_a), F32)] * (N_ROWS - len(rows))
    pvec = jnp.stack(rows, axis=1)
    gvec = jnp.stack([pre_gain, post_gain] + [jnp.zeros_like(pre_gain)] * 6, axis=1)
    zl = jnp.zeros((depth, LORA, d_a), F32)
    wl = jnp.concatenate([jnp.concatenate([w2, zl], axis=-1),
                          jnp.concatenate([zl, a2], axis=-1)], axis=1).astype(BF16)
    p_a_b, p_b_b, w_out_b = p_a.astype(BF16), p_b.astype(BF16), w_out.astype(BF16)

    c_rows = 8 * pl.cdiv(bsz, 8)
    c_pad = jnp.zeros((c_rows, d), F32).at[:bsz].set(c)
    t = _tiles(bsz, seq, d, d_a)
    mod = _ada_mod(c_pad, ada_w, ada_b.reshape(depth, 1, 3 * d), tn=t["ada_tn"])
    mod4 = mod[:, :bsz].reshape(depth, bsz, 3, d)

    x2 = x.reshape(bsz * seq, d)
    for layer in range(depth):
        proj, lx = _proj(x2, mod4, gvec, w_in, layer, seq, t["proj_tm"], t["proj_tn"], d_a)
        ya = _wkv(proj.reshape(bsz, seq, -1), lx.reshape(bsz, seq, -1), pvec, wl, layer, d_a,
                  t["wkv_rows"])
        x2 = _out_stage(x2, ya.reshape(bsz * seq, d_a), proj, mod4, gvec, pvec,
                        p_a_b, p_b_b, w_out_b, layer, seq, t["out_tm"], d_a)
    return x2.reshape(bsz, seq, d)
```

```python
import functools
import math

import jax
import jax.numpy as jnp
from jax import lax
from jax.experimental import pallas as pl
from jax.experimental.pallas import tpu as pltpu

HEAD_SIZE = 64
LORA = 64
RMS_EPS = 1e-6
GN_EPS = 64e-5
CHUNK = 64
SUB = 32
LANES = 128
VMEM_LIMIT = 56 * 1024 * 1024
LOG2_E = math.log2(math.e)

F32 = jnp.float32
BF16 = jnp.bfloat16

(ROW_W0, ROW_A0, ROW_KK, ROW_KA, ROW_RK, ROW_LNG, ROW_LNB, ROW_MUR, ROW_MUK, ROW_MUV,
 ROW_MUL, ROW_CW0, ROW_CW1, ROW_CW2) = range(14)
N_ROWS = 16


def _dot(a, b):
    return jnp.dot(a, b, preferred_element_type=F32)


def _dot_nt(a, b):
    return lax.dot_general(a, b, (((1,), (1,)), ((), ())), preferred_element_type=F32)


def _dot_tn(a, b):
    return lax.dot_general(a, b, (((0,), (0,)), ((), ())), preferred_element_type=F32)


def _split_dot(x, m):
    hi = x.astype(BF16)
    lo = (x - hi.astype(F32)).astype(BF16)
    return _dot(hi, m) + _dot(lo, m)


def _split_dot_left(m, x):
    hi = x.astype(BF16)
    lo = (x - hi.astype(F32)).astype(BF16)
    return _dot(m, hi) + _dot(m, lo)


def _sigmoid(x):
    return jax.nn.sigmoid(x)


def _ada_kernel(c_ref, w_ref, b_ref, o_ref):
    c = c_ref[...]
    c_act = c * _sigmoid(c)
    w = w_ref[...]
    w_hi = w.astype(BF16)
    w_lo = (w - w_hi.astype(F32)).astype(BF16)
    o_ref[...] = _split_dot(c_act, w_hi) + _dot(c_act.astype(BF16), w_lo) + b_ref[...]


def _ada_mod(c_pad, ada_w, ada_b3, tn):
    depth, d, n3 = ada_w.shape
    rows = c_pad.shape[0]
    return pl.pallas_call(
        _ada_kernel,
        name="ada_mod",
        grid=(depth, pl.cdiv(n3, tn)),
        in_specs=[
            pl.BlockSpec((rows, d), lambda l, j: (0, 0)),
            pl.BlockSpec((None, d, tn), lambda l, j: (l, 0, j)),
            pl.BlockSpec((None, 1, tn), lambda l, j: (l, 0, j)),
        ],
        out_specs=pl.BlockSpec((None, rows, tn), lambda l, j: (l, 0, j)),
        out_shape=jax.ShapeDtypeStruct((depth, rows, n3), F32),
        compiler_params=pltpu.CompilerParams(
            dimension_semantics=("arbitrary", "arbitrary"), vmem_limit_bytes=VMEM_LIMIT),
    )(c_pad, ada_w, ada_b3)


def _proj_kernel(x_ref, mod_ref, g_ref, w_ref, wl_ref, o_ref, ol_ref, h_ref):
    @pl.when(pl.program_id(1) == 0)
    def _():
        x = x_ref[...]
        ms = jnp.mean(x * x, axis=-1, keepdims=True)
        y = x * lax.rsqrt(ms + RMS_EPS) * g_ref[0:1, :]
        h = (y * (1.0 + mod_ref[1:2, :]) + mod_ref[0:1, :]).astype(BF16)
        h_ref[...] = h
        ol_ref[...] = _dot(h, wl_ref[...].astype(BF16)).astype(BF16)

    o_ref[...] = _dot(h_ref[...], w_ref[...].astype(BF16)).astype(BF16)


def _proj(x2, mod4, gvec, w_in, layer, seq, tm, tn, d_a):
    bt, d = x2.shape
    rows_per_seq = seq // tm
    cols_a = 3 * d_a + 2 * LORA
    off_b = cols_a + d_a
    off_gate = off_b + 4 * d_a
    n_main = 12 * d_a
    assert d_a % tn == 0

    def w_col(j):
        new = j * tn
        old = jnp.where(new < 4 * d_a, new + off_gate,
                        jnp.where(new < 8 * d_a, new + (off_b - 4 * d_a),
                                  jnp.where(new < 9 * d_a, new + (cols_a - 8 * d_a),
                                            new - 9 * d_a)))
        return pl.multiple_of(old, LANES)

    return pl.pallas_call(
        _proj_kernel,
        name="in_proj",
        grid=(bt // tm, n_main // tn),
        in_specs=[
            pl.BlockSpec((tm, d), lambda i, j: (i, 0)),
            pl.BlockSpec((None, None, 3, d), lambda i, j: (layer, i // rows_per_seq, 0, 0)),
            pl.BlockSpec((None, 8, d), lambda i, j: (layer, 0, 0)),
            pl.BlockSpec((pl.Squeezed(), pl.Element(d), pl.Element(tn)),
                         lambda i, j: (layer, 0, w_col(j))),
            pl.BlockSpec((pl.Squeezed(), pl.Element(d), pl.Element(2 * LORA)),
                         lambda i, j: (layer, 0, 3 * d_a)),
        ],
        out_specs=[pl.BlockSpec((tm, tn), lambda i, j: (i, j)),
                   pl.BlockSpec((tm, 2 * LORA), lambda i, j: (i, 0))],
        out_shape=[jax.ShapeDtypeStruct((bt, n_main), BF16),
                   jax.ShapeDtypeStruct((bt, 2 * LORA), BF16)],
        scratch_shapes=[pltpu.VMEM((tm, d), BF16)],
        compiler_params=pltpu.CompilerParams(
            dimension_semantics=("parallel", "arbitrary"), vmem_limit_bytes=VMEM_LIMIT),
    )(x2, mod4, gvec, w_in, w_in)


def _wkv_kernel(r_ref, k_ref, v_ref, lx_ref, z_ref, pv_ref, wl_ref, o_ref,
                s_ref, pr_ref, pk_ref, pvp_ref, pl_ref, lora_ref,
                a_st, r_st, b_st, k_st, bd_st, kd_st, v_st, bonus_st, wmid_st, wend_st):
    step = pl.program_id(1)
    nb, C, d_a = r_ref.shape
    n_pairs = d_a // LANES
    units = [(b, hp) for b in range(nb) for hp in range(n_pairs)]
    n_units = len(units)
    slot_w = step % 2
    slot_r = 1 - slot_w

    @pl.when(step == 0)
    def _():
        s_ref[...] = jnp.zeros_like(s_ref)
        pr_ref[...] = jnp.zeros_like(pr_ref)
        pk_ref[...] = jnp.zeros_like(pk_ref)
        pvp_ref[...] = jnp.zeros_like(pvp_ref)
        pl_ref[...] = jnp.zeros_like(pl_ref)
        for ref in (a_st, r_st, b_st, k_st, bd_st, kd_st, v_st, bonus_st, wmid_st, wend_st):
            ref[1] = jnp.zeros(ref.shape[1:], ref.dtype)

    row = lax.broadcasted_iota(jnp.int32, (C, LANES), 0)
    lane = lax.broadcasted_iota(jnp.int32, (C, LANES), 1)
    col = lane & (HEAD_SIZE - 1)
    head0 = lane < HEAD_SIZE
    strict = col < row
    incl = col <= row
    same_sub = (col // SUB) == (row // SUB)
    eye = jnp.where(col == row, 1.0, 0.0).astype(F32)
    eye_c = jnp.where((lax.broadcasted_iota(jnp.int32, (SUB, LANES), 1) & (SUB - 1))
                      == lax.broadcasted_iota(jnp.int32, (SUB, LANES), 0), 1.0, 0.0).astype(F32)
    first_row = row == 0

    sq_r = lax.broadcasted_iota(jnp.int32, (LANES, LANES), 0)
    sq_c = lax.broadcasted_iota(jnp.int32, (LANES, LANES), 1)
    same_head = (sq_r < HEAD_SIZE) == (sq_c < HEAD_SIZE)
    same_sub_sq = (sq_r // SUB) == (sq_c // SUB)
    ones_bd = jnp.where(same_head, 1.0, 0.0).astype(BF16)
    mean_bd = jnp.where(same_head, 1.0 / HEAD_SIZE, 0.0).astype(BF16)
    tr = lax.broadcasted_iota(jnp.int32, (C, C), 0)
    tc = lax.broadcasted_iota(jnp.int32, (C, C), 1)
    tril = jnp.where(tc <= tr, 1.0, 0.0).astype(BF16)

    def lanes_of(hp):
        return slice(hp * LANES, (hp + 1) * LANES)

    def shifted(x, prev):
        return jnp.where(first_row, prev, pltpu.roll(x, 1, 0))

    def stack2(x):
        z = jnp.zeros_like(x)
        return jnp.concatenate([jnp.where(head0, x, z), jnp.where(head0, z, x)], axis=0)

    def mm(x, y_stacked):
        return _dot(x.astype(BF16), y_stacked)

    def seg_sum(x):
        return _dot(x.astype(BF16), ones_bd)

    def prep_lora():
        lora_in = []
        for b in range(nb):
            lx = lx_ref[b].astype(F32)
            lx_s = lx + (shifted(lx, pl_ref[b]) - lx) * pv_ref[ROW_MUL:ROW_MUL + 1, 0:LANES]
            lora_in.append(jnp.where(head0, jnp.tanh(lx_s), lx_s).astype(BF16))
            pl_ref[b] = lx[C - 1:C, :]
        lora = _dot(jnp.concatenate(lora_in, axis=0), wl_ref[...])
        for b in range(nb):
            lora_ref[b] = lora[b * C:(b + 1) * C]

    def prep_unit(b, hp):
        sl = lanes_of(hp)
        par = lambda r_: pv_ref[r_:r_ + 1, sl]
        r_raw = r_ref[b, :, sl].astype(F32)
        k_raw = k_ref[b, :, sl].astype(F32)
        v_raw = v_ref[b, :, sl].astype(F32)
        r = r_raw + (shifted(r_raw, pr_ref[b, :, sl]) - r_raw) * par(ROW_MUR)
        k = k_raw + (shifted(k_raw, pk_ref[b, :, sl]) - k_raw) * par(ROW_MUK)
        v = v_raw + (shifted(v_raw, pvp_ref[b, :, sl]) - v_raw) * par(ROW_MUV)
        pr_ref[b, :, sl] = r_raw[C - 1:C, :]
        pk_ref[b, :, sl] = k_raw[C - 1:C, :]
        pvp_ref[b, :, sl] = v_raw[C - 1:C, :]

        logw = (-math.exp(-0.5) * LOG2_E) * _sigmoid(par(ROW_W0) + lora_ref[b, :, sl])
        lr = _sigmoid(par(ROW_A0) + lora_ref[b, :, d_a + hp * LANES:d_a + (hp + 1) * LANES])

        kk = k * par(ROW_KK)
        kmod = k * (1.0 + (lr - 1.0) * par(ROW_KA))
        sums = seg_sum(jnp.concatenate([kk * kk, r * kmod * par(ROW_RK)], axis=0))
        kk = kk * lax.rsqrt(jnp.maximum(sums[:C], 1e-24))
        bvec = kk * lr
        bonus_st[slot_w, b, :, sl] = sums[C:] * v

        p_inc = _split_dot_left(tril, logw)
        p_mid = p_inc[C // 2 - 1: C // 2, :]
        p_end = p_inc[C - 1: C, :]
        pc = p_inc - p_mid
        e_inv = jnp.exp2(-pc)
        e_out = jnp.exp2(p_end - p_inc)
        a_st[slot_w, b, :, sl] = (-kk * jnp.exp2(pc - logw)).astype(BF16)
        r_st[slot_w, b, :, sl] = (r * jnp.exp2(pc)).astype(BF16)
        b_st[slot_w, b, :, sl] = (bvec * e_inv).astype(BF16)
        k_st[slot_w, b, :, sl] = (kmod * e_inv).astype(BF16)
        bd_st[slot_w, b, :, sl] = (bvec * e_out).astype(BF16)
        kd_st[slot_w, b, :, sl] = (kmod * e_out).astype(BF16)
        v_st[slot_w, b, :, sl] = v.astype(BF16)
        wmid_st[slot_w, b, :, sl] = jnp.exp2(p_mid)
        wend_st[slot_w, b, :, sl] = jnp.exp2(p_end)

    env = {}

    def ld(ref, b, hp):
        return ref[slot_r, b, :, lanes_of(hp)]

    def st_scores():
        env["ar"] = [jnp.concatenate([ld(a_st, b, hp), ld(r_st, b, hp)], axis=0) for b, hp in units]
        bk = [jnp.concatenate([stack2(ld(b_st, b, hp)), stack2(ld(k_st, b, hp))], axis=0)
              for b, hp in units]
        env["a_all"] = [_dot_nt(env["ar"][u], bk[u]) for u in range(n_units)]

    def st_state_in():
        env["s_prev"] = [s_ref[u] for u in range(n_units)]
        s_hat = [(env["s_prev"][u] * ld(wmid_st, b, hp)).astype(BF16)
                 for u, (b, hp) in enumerate(units)]
        env["ar_s"] = [_dot_nt(env["ar"][u], s_hat[u]) for u in range(n_units)]
        a_all = env["a_all"]
        env["a_ab"] = [jnp.where(strict, a[:C, :LANES], 0.0) for a in a_all]
        env["a_rb"] = [jnp.where(incl, a[C:, :LANES], 0.0) for a in a_all]
        a_k = [jnp.concatenate([jnp.where(strict, a[:C, LANES:], 0.0),
                                jnp.where(incl, a[C:, LANES:], 0.0)], axis=0) for a in a_all]
        env["a_v"] = [mm(a_k[u], stack2(ld(v_st, b, hp))) for u, (b, hp) in enumerate(units)]
        env["rhs"] = [env["ar_s"][u][:C] + env["a_v"][u][:C] for u in range(n_units)]

    def diag_rhs(x_c):
        return jnp.where(same_sub_sq, jnp.concatenate([x_c] * (LANES // SUB), axis=0),
                         0.0).astype(BF16)

    def fused_step(p, l_pow):
        x = mm(jnp.concatenate([p, l_pow], axis=0), diag_rhs(l_pow))
        return p + x[:SUB], x[SUB:]

    def st_t1():
        a_ab = env["a_ab"]
        l_d = [jnp.where(same_sub, a, 0.0) for a in a_ab]
        env["l_o"] = [stack2((a_ab[u] - l_d[u]).astype(BF16)) for u in range(n_units)]
        l_c = [sum(a[i * SUB:(i + 1) * SUB] for i in range(1, C // SUB)) + a[:SUB] for a in l_d]
        env["l_pow"] = [mm(l_c[u], diag_rhs(l_c[u])) for u in range(n_units)]
        env["p"] = [eye_c + a for a in l_c]

    def st_t2():
        res = [fused_step(env["p"][u], env["l_pow"][u]) for u in range(n_units)]
        env["p"] = [r_[0] for r_ in res]
        env["l_pow"] = [r_[1] for r_ in res]

    def st_t4():
        p, l8 = env["p"], env["l_pow"]
        t_c = [p[u] + mm(p[u], diag_rhs(l8[u])) for u in range(n_units)]
        env["t_d"] = [jnp.where(same_sub, jnp.concatenate([t] * (C // SUB), axis=0), 0.0)
                      for t in t_c]

    def st_n1():
        env["n_pow"] = [mm(env["t_d"][u], env["l_o"][u]) for u in range(n_units)]
        env["q"] = [eye + n for n in env["n_pow"]]

    def st_n_square():
        n = env["n_pow"]
        env["n_pow"] = [mm(n[u], stack2(n[u].astype(BF16))) for u in range(n_units)]

    def st_q():
        q, n = env["q"], env["n_pow"]
        env["q"] = [q[u] + mm(q[u], stack2(n[u].astype(BF16))) for u in range(n_units)]

    def st_tinv():
        env["t_inv"] = [mm(env["q"][u], stack2(env["t_d"][u].astype(BF16))) for u in range(n_units)]

    def st_u():
        env["u_b"] = [mm(env["t_inv"][u], stack2(env["rhs"][u].astype(BF16))).astype(BF16)
                      for u in range(n_units)]

    def st_y_state():
        u_b = env["u_b"]
        env["y"] = [env["ar_s"][u][C:] + env["a_v"][u][C:] + mm(env["a_rb"][u], stack2(u_b[u]))
                    for u in range(n_units)]
        for u, (b, hp) in enumerate(units):
            uv = jnp.concatenate([u_b[u], ld(v_st, b, hp)], axis=0)
            bk_d = jnp.concatenate([ld(bd_st, b, hp), ld(kd_st, b, hp)], axis=0)
            s_new = env["s_prev"][u] * ld(wend_st, b, hp) + _dot_tn(uv, bk_d)
            s_ref[u] = jnp.where(same_head, s_new, 0.0)

    def st_mean():
        y = jnp.concatenate(env["y"], axis=0)
        env["yc"] = y - _dot(y.astype(BF16), mean_bd)

    def st_out():
        yc = env["yc"]
        var = _dot((yc * yc).astype(BF16), mean_bd)
        yn_all = yc * lax.rsqrt(var + GN_EPS)
        for u, (b, hp) in enumerate(units):
            sl = lanes_of(hp)
            yn = yn_all[u * C:(u + 1) * C] * pv_ref[ROW_LNG:ROW_LNG + 1, sl] + pv_ref[ROW_LNB:ROW_LNB + 1, sl]
            z = z_ref[b, :, sl].astype(F32)
            o_ref[b, :, sl] = ((yn + ld(bonus_st, b, hp)) * (z * _sigmoid(z))).astype(BF16)

    n_diag = SUB.bit_length() - 3
    n_rem = (C // SUB).bit_length() - 2
    stages = ([st_scores, st_state_in, st_t1] + [st_t2] * n_diag + [st_t4, st_n1]
              + [st_n_square, st_q] * n_rem + [st_tinv, st_u, st_y_state, st_mean, st_out])

    prep_lora()
    per_stage = -(-n_units // (len(stages) - 1))
    todo = list(units)
    for stage in stages:
        stage()
        for b, hp in todo[:per_stage]:
            prep_unit(b, hp)
        todo = todo[per_stage:]
    assert not todo


def _wkv(proj3, lx3, pvec, wl, layer, d_a, nb):
    bsz, seq, _ = proj3.shape
    C = CHUNK
    n_chunks = seq // C
    n_pairs = d_a // LANES
    cur = lambda idx: (lambda b, s: (b, jnp.minimum(s, n_chunks - 1), idx))
    prev = lambda idx: (lambda b, s: (b, jnp.maximum(s - 1, 0), idx))
    stash = lambda dt: pltpu.VMEM((2, nb, C, d_a), dt)
    return pl.pallas_call(
        _wkv_kernel,
        name="wkv7_chunked",
        grid=(bsz // nb, n_chunks + 1),
        in_specs=[
            pl.BlockSpec((nb, C, d_a), cur(9)),
            pl.BlockSpec((nb, C, d_a), cur(10)),
            pl.BlockSpec((nb, C, d_a), cur(11)),
            pl.BlockSpec((nb, C, LANES), cur(0)),
            pl.BlockSpec((nb, C, d_a), prev(8)),
            pl.BlockSpec((None, N_ROWS, d_a), lambda b, s: (layer, 0, 0)),
            pl.BlockSpec((None, LANES, 2 * d_a), lambda b, s: (layer, 0, 0)),
        ],
        out_specs=pl.BlockSpec((nb, C, d_a), prev(0)),
        out_shape=jax.ShapeDtypeStruct((bsz, seq, d_a), BF16),
        scratch_shapes=[
            pltpu.VMEM((nb * n_pairs, LANES, LANES), F32),
            pltpu.VMEM((nb, 1, d_a), F32),
            pltpu.VMEM((nb, 1, d_a), F32),
            pltpu.VMEM((nb, 1, d_a), F32),
            pltpu.VMEM((nb, 1, LANES), F32),
            pltpu.VMEM((nb, C, 2 * d_a), F32),
            stash(BF16), stash(BF16), stash(BF16), stash(BF16), stash(BF16), stash(BF16),
            stash(BF16), stash(F32),
            pltpu.VMEM((2, nb, 1, d_a), F32), pltpu.VMEM((2, nb, 1, d_a), F32),
        ],
        compiler_params=pltpu.CompilerParams(
            dimension_semantics=("parallel", "arbitrary"), vmem_limit_bytes=VMEM_LIMIT),
    )(proj3, proj3, proj3, lx3, proj3, pvec, wl)


def _out_kernel(x_ref, ya_ref, ga_ref, gb_ref, bg_ref, cg_ref, hb_ref, zb_ref, cgh_ref, hbh_ref,
                mod_ref, g_ref, pv_ref, pa_ref, pb_ref, wo_ref, o_ref, *, tiles_per_seq):
    tm = x_ref.shape[0]
    i = pl.program_id(0)
    starts_seq = (i % tiles_per_seq) == 0

    u = cg_ref[...].astype(F32) * hb_ref[...].astype(F32)
    halo = cgh_ref[...].astype(F32) * hbh_ref[...].astype(F32)
    halo = jnp.where(starts_seq, 0.0, halo)
    row = lax.broadcasted_iota(jnp.int32, u.shape, 0)
    u1 = jnp.where(row == 0, halo[7:8, :], pltpu.roll(u, 1, 0))
    u2 = jnp.where(row == 0, halo[6:7, :], jnp.where(row == 1, halo[7:8, :], pltpu.roll(u, 2, 0)))
    conv = (pv_ref[ROW_CW0:ROW_CW0 + 1, :] * u2 + pv_ref[ROW_CW1:ROW_CW1 + 1, :] * u1
            + pv_ref[ROW_CW2:ROW_CW2 + 1, :] * u)
    zb = zb_ref[...].astype(F32)
    yb_in = bg_ref[...].astype(F32) * conv * (zb * _sigmoid(zb))

    y_a = _dot(ya_ref[...], pa_ref[...])
    y_b = _dot(yb_in.astype(BF16), pb_ref[...])
    m = _sigmoid(ga_ref[...].astype(F32)) * y_a + _sigmoid(gb_ref[...].astype(F32)) * y_b
    o = _dot(m.astype(BF16), wo_ref[...])
    ms = jnp.mean(o * o, axis=-1, keepdims=True)
    o_n = o * lax.rsqrt(ms + RMS_EPS) * g_ref[1:2, :]
    o_ref[...] = x_ref[...] + mod_ref[2:3, :] * o_n


def _out_stage(x2, ya2, proj, mod4, gvec, pvec, p_a, p_b, w_out, layer, seq, tm, d_a):
    bt, d = x2.shape
    tiles_per_seq = seq // tm
    hb8 = tm // 8
    const = lambda i: (layer, 0, 0)
    colblk = lambda idx: (lambda i: (i, idx))
    halo = lambda idx: (lambda i: (jnp.maximum(i * hb8 - 1, 0), idx))
    single = pl.Buffered(1)
    return pl.pallas_call(
        functools.partial(_out_kernel, tiles_per_seq=tiles_per_seq),
        name="out_stage",
        grid=(bt // tm,),
        in_specs=[
            pl.BlockSpec((tm, d), lambda i: (i, 0)),
            pl.BlockSpec((tm, d_a), lambda i: (i, 0)),
            pl.BlockSpec((tm, d), colblk(0)),
            pl.BlockSpec((tm, d), colblk(1)),
            pl.BlockSpec((tm, d_a), colblk(4)),
            pl.BlockSpec((tm, d_a), colblk(5)),
            pl.BlockSpec((tm, d_a), colblk(6)),
            pl.BlockSpec((tm, d_a), colblk(7)),
            pl.BlockSpec((8, d_a), halo(5)),
            pl.BlockSpec((8, d_a), halo(6)),
            pl.BlockSpec((None, None, 3, d), lambda i: (layer, i // tiles_per_seq, 0, 0)),
            pl.BlockSpec((None, 8, d), const),
            pl.BlockSpec((None, N_ROWS, d_a), const),
            pl.BlockSpec((None, d_a, d), const, pipeline_mode=single),
            pl.BlockSpec((None, d_a, d), const, pipeline_mode=single),
            pl.BlockSpec((None, d, d), const, pipeline_mode=single),
        ],
        out_specs=pl.BlockSpec((tm, d), lambda i: (i, 0)),
        out_shape=jax.ShapeDtypeStruct((bt, d), F32),
        compiler_params=pltpu.CompilerParams(
            dimension_semantics=("parallel",), vmem_limit_bytes=VMEM_LIMIT),
    )(x2, ya2, proj, proj, proj, proj, proj, proj, proj, proj, mod4, gvec, pvec, p_a, p_b, w_out)


def _tiles(bsz, seq, d, d_a):
    f32, bf16 = 4, 2
    proj_tm = min(1024, seq)
    proj_tn = min(1024, d_a)
    proj_bytes = (2 * proj_tm * d * f32 + proj_tm * d * bf16
                  + 2 * d * proj_tn * f32 + d * proj_tn * bf16
                  + 2 * proj_tm * proj_tn * bf16)
    out_tm = min(256, seq)
    out_bytes = ((d_a * d * 2 + d * d) * bf16
                 + 2 * out_tm * (2 * d * f32 + (5 * d_a + 2 * d) * bf16)
                 + 5 * out_tm * d * f32)
    assert max(proj_bytes, out_bytes) < VMEM_LIMIT
    wkv_rows = 4 if bsz % 4 == 0 else (2 if bsz % 2 == 0 else 1)
    return dict(ada_tn=min(1024, 3 * d), proj_tm=proj_tm, proj_tn=proj_tn, out_tm=out_tm,
                wkv_rows=wkv_rows)


def kernel(x, c, ada_w, ada_b, pre_gain, post_gain, w_in, mu_shift, w0, w2, a0, a2, k_k, k_a, r_k,
           lnx_gain, lnx_bias, conv_w, p_a, p_b, w_out):
    bsz, seq, d = x.shape
    depth = ada_w.shape[0]
    d_a = w0.shape[-1]
    assert d == 2 * d_a and d_a % LANES == 0 and seq % CHUNK == 0 and 2 * LORA == LANES
    assert w2.shape[1] == LORA and a2.shape[1] == LORA

    pad_l = jnp.zeros((depth, d_a - 2 * LORA), F32)
    rows = [w0, a0, k_k, k_a, r_k.reshape(depth, d_a), lnx_gain, lnx_bias,
            mu_shift[:, :d_a], mu_shift[:, d_a:2 * d_a], mu_shift[:, 2 * d_a:3 * d_a],
            jnp.concatenate([mu_shift[:, 3 * d_a:], pad_l], axis=-1),
            conv_w[:, 0], conv_w[:, 1], conv_w[:, 2]]
    rows += [jnp.zeros((depth, d_a), F32)] * (N_ROWS - len(rows))
    pvec = jnp.stack(rows, axis=1)
    gvec = jnp.stack([pre_gain, post_gain] + [jnp.zeros_like(pre_gain)] * 6, axis=1)
    zl = jnp.zeros((depth, LORA, d_a), F32)
    wl = jnp.concatenate([jnp.concatenate([w2, zl], axis=-1),
                          jnp.concatenate([zl, a2], axis=-1)], axis=1).astype(BF16)
    p_a_b, p_b_b, w_out_b = p_a.astype(BF16), p_b.astype(BF16), w_out.astype(BF16)

    c_rows = 8 * pl.cdiv(bsz, 8)
    c_pad = jnp.zeros((c_rows, d), F32).at[:bsz].set(c)
    t = _tiles(bsz, seq, d, d_a)
    mod = _ada_mod(c_pad, ada_w, ada_b.reshape(depth, 1, 3 * d), tn=t["ada_tn"])
    mod4 = mod[:, :bsz].reshape(depth, bsz, 3, d)

    x2 = x.reshape(bsz * seq, d)
    for layer in range(depth):
        proj, lx = _proj(x2, mod4, gvec, w_in, layer, seq, t["proj_tm"], t["proj_tn"], d_a)
        ya = _wkv(proj.reshape(bsz, seq, -1), lx.reshape(bsz, seq, -1), pvec, wl, layer, d_a,
                  t["wkv_rows"])
        x2 = _out_stage(x2, ya.reshape(bsz * seq, d_a), proj, mod4, gvec, pvec,
                        p_a_b, p_b_b, w_out_b, layer, seq, t["out_tm"], d_a)
    return x2.reshape(bsz, seq, d)
```

```python
import functools
import math

import jax
import jax.numpy as jnp
from jax import lax
from jax.experimental import pallas as pl
from jax.experimental.pallas import tpu as pltpu

HEAD_SIZE = 64
LORA = 64
RMS_EPS = 1e-6
GN_EPS = 64e-5
CHUNK = 64
SUB = 32
LANES = 128
VMEM_LIMIT = 56 * 1024 * 1024
LOG2_E = math.log2(math.e)

F32 = jnp.float32
BF16 = jnp.bfloat16

(ROW_W0, ROW_A0, ROW_KK, ROW_KA, ROW_RK, ROW_LNG, ROW_LNB, ROW_MUR, ROW_MUK, ROW_MUV,
 ROW_MUL, ROW_CW0, ROW_CW1, ROW_CW2) = range(14)
N_ROWS = 16


def _dot(a, b):
    return jnp.dot(a, b, preferred_element_type=F32)


def _dot_nt(a, b):
    return lax.dot_general(a, b, (((1,), (1,)), ((), ())), preferred_element_type=F32)


def _dot_tn(a, b):
    return lax.dot_general(a, b, (((0,), (0,)), ((), ())), preferred_element_type=F32)


def _split_dot(x, m):
    hi = x.astype(BF16)
    lo = (x - hi.astype(F32)).astype(BF16)
    return _dot(hi, m) + _dot(lo, m)


def _split_dot_left(m, x):
    hi = x.astype(BF16)
    lo = (x - hi.astype(F32)).astype(BF16)
    return _dot(m, hi) + _dot(m, lo)


def _sigmoid(x):
    return jax.nn.sigmoid(x)


ADA_STEPS = 16


def _ada_kernel(c_ref, w_ref, b_ref, pa_ref, pb_ref, wo_ref, o_ref, pa_o, pb_o, wo_o):
    c = c_ref[...]
    c_act = c * _sigmoid(c)
    w = w_ref[...]
    w_hi = w.astype(BF16)
    w_lo = (w - w_hi.astype(F32)).astype(BF16)
    o_ref[...] = _split_dot(c_act, w_hi) + _dot(c_act.astype(BF16), w_lo) + b_ref[...]

    j = pl.program_id(1)
    quarter = ADA_STEPS // 4

    @pl.when(j < quarter)
    def _():
        pa_o[...] = pa_ref[...].astype(BF16)

    @pl.when((j >= quarter) & (j < 2 * quarter))
    def _():
        pb_o[...] = pb_ref[...].astype(BF16)

    @pl.when(j >= 2 * quarter)
    def _():
        wo_o[...] = wo_ref[...].astype(BF16)


def _ada_mod(c_pad, ada_w, ada_b3, p_a, p_b, w_out):
    depth, d, n3 = ada_w.shape
    d_a = p_a.shape[1]
    rows = c_pad.shape[0]
    quarter = ADA_STEPS // 4
    tn = n3 // ADA_STEPS
    rb = d_a // quarter
    assert n3 % (ADA_STEPS * LANES) == 0 and d == 2 * d_a and rb % 16 == 0
    blk = lambda lo, n: (lambda l, j: (l, jnp.clip(j - lo, 0, n - 1), 0))
    cast_specs = [pl.BlockSpec((None, rb, d), blk(0, quarter)),
                  pl.BlockSpec((None, rb, d), blk(quarter, quarter)),
                  pl.BlockSpec((None, rb, d), blk(2 * quarter, 2 * quarter))]
    return pl.pallas_call(
        _ada_kernel,
        name="ada_mod",
        grid=(depth, ADA_STEPS),
        in_specs=[
            pl.BlockSpec((rows, d), lambda l, j: (0, 0)),
            pl.BlockSpec((None, d, tn), lambda l, j: (l, 0, j)),
            pl.BlockSpec((None, 1, tn), lambda l, j: (l, 0, j)),
        ] + cast_specs,
        out_specs=[pl.BlockSpec((None, rows, tn), lambda l, j: (l, 0, j))] + cast_specs,
        out_shape=[jax.ShapeDtypeStruct((depth, rows, n3), F32),
                   jax.ShapeDtypeStruct(p_a.shape, BF16),
                   jax.ShapeDtypeStruct(p_b.shape, BF16),
                   jax.ShapeDtypeStruct(w_out.shape, BF16)],
        compiler_params=pltpu.CompilerParams(
            dimension_semantics=("arbitrary", "arbitrary"), vmem_limit_bytes=VMEM_LIMIT),
    )(c_pad, ada_w, ada_b3, p_a, p_b, w_out)


def _proj_kernel(x_ref, mod_ref, g_ref, w_ref, wl_ref, o_ref, ol_ref, h_ref):
    @pl.when(pl.program_id(1) == 0)
    def _():
        x = x_ref[...]
        ms = jnp.mean(x * x, axis=-1, keepdims=True)
        y = x * lax.rsqrt(ms + RMS_EPS) * g_ref[0:1, :]
        h = (y * (1.0 + mod_ref[1:2, :]) + mod_ref[0:1, :]).astype(BF16)
        h_ref[...] = h
        ol_ref[...] = _dot(h, wl_ref[...].astype(BF16)).astype(BF16)

    o_ref[...] = _dot(h_ref[...], w_ref[...].astype(BF16)).astype(BF16)


def _proj(x2, mod4, gvec, w_in, layer, seq, tm, tn, d_a):
    bt, d = x2.shape
    rows_per_seq = seq // tm
    cols_a = 3 * d_a + 2 * LORA
    off_b = cols_a + d_a
    off_gate = off_b + 4 * d_a
    n_main = 12 * d_a
    assert d_a % tn == 0

    def w_col(j):
        new = j * tn
        old = jnp.where(new < 4 * d_a, new + off_gate,
                        jnp.where(new < 8 * d_a, new + (off_b - 4 * d_a),
                                  jnp.where(new < 9 * d_a, new + (cols_a - 8 * d_a),
                                            new - 9 * d_a)))
        return pl.multiple_of(old, LANES)

    return pl.pallas_call(
        _proj_kernel,
        name="in_proj",
        grid=(bt // tm, n_main // tn),
        in_specs=[
            pl.BlockSpec((tm, d), lambda i, j: (i, 0)),
            pl.BlockSpec((None, None, 3, d), lambda i, j: (layer, i // rows_per_seq, 0, 0)),
            pl.BlockSpec((None, 8, d), lambda i, j: (layer, 0, 0)),
            pl.BlockSpec((pl.Squeezed(), pl.Element(d), pl.Element(tn)),
                         lambda i, j: (layer, 0, w_col(j))),
            pl.BlockSpec((pl.Squeezed(), pl.Element(d), pl.Element(2 * LORA)),
                         lambda i, j: (layer, 0, 3 * d_a)),
        ],
        out_specs=[pl.BlockSpec((tm, tn), lambda i, j: (i, j)),
                   pl.BlockSpec((tm, 2 * LORA), lambda i, j: (i, 0))],
        out_shape=[jax.ShapeDtypeStruct((bt, n_main), BF16),
                   jax.ShapeDtypeStruct((bt, 2 * LORA), BF16)],
        scratch_shapes=[pltpu.VMEM((tm, d), BF16)],
        compiler_params=pltpu.CompilerParams(
            dimension_semantics=("parallel", "arbitrary"), vmem_limit_bytes=VMEM_LIMIT),
    )(x2, mod4, gvec, w_in, w_in)


def _wkv_kernel(r_ref, k_ref, v_ref, lx_ref, z_ref, pv_ref, wl_ref, o_ref,
                s_ref, pr_ref, pk_ref, pvp_ref, pl_ref, lora_ref,
                a_st, r_st, b_st, k_st, bd_st, kd_st, v_st, bonus_st, wmid_st, wend_st):
    step = pl.program_id(1)
    nb, C, d_a = r_ref.shape
    n_pairs = d_a // LANES
    units = [(b, hp) for b in range(nb) for hp in range(n_pairs)]
    n_units = len(units)
    slot_w = step % 2
    slot_r = 1 - slot_w

    @pl.when(step == 0)
    def _():
        s_ref[...] = jnp.zeros_like(s_ref)
        pr_ref[...] = jnp.zeros_like(pr_ref)
        pk_ref[...] = jnp.zeros_like(pk_ref)
        pvp_ref[...] = jnp.zeros_like(pvp_ref)
        pl_ref[...] = jnp.zeros_like(pl_ref)
        for ref in (a_st, r_st, b_st, k_st, bd_st, kd_st, v_st, bonus_st, wmid_st, wend_st):
            ref[1] = jnp.zeros(ref.shape[1:], ref.dtype)

    row = lax.broadcasted_iota(jnp.int32, (C, LANES), 0)
    lane = lax.broadcasted_iota(jnp.int32, (C, LANES), 1)
    col = lane & (HEAD_SIZE - 1)
    head0 = lane < HEAD_SIZE
    strict = col < row
    incl = col <= row
    same_sub = (col // SUB) == (row // SUB)
    eye = jnp.where(col == row, 1.0, 0.0).astype(F32)
    eye_c = jnp.where((lax.broadcasted_iota(jnp.int32, (SUB, LANES), 1) & (SUB - 1))
                      == lax.broadcasted_iota(jnp.int32, (SUB, LANES), 0), 1.0, 0.0).astype(F32)
    first_row = row == 0

    sq_r = lax.broadcasted_iota(jnp.int32, (LANES, LANES), 0)
    sq_c = lax.broadcasted_iota(jnp.int32, (LANES, LANES), 1)
    same_head = (sq_r < HEAD_SIZE) == (sq_c < HEAD_SIZE)
    same_sub_sq = (sq_r // SUB) == (sq_c // SUB)
    ones_bd = jnp.where(same_head, 1.0, 0.0).astype(BF16)
    mean_bd = jnp.where(same_head, 1.0 / HEAD_SIZE, 0.0).astype(BF16)
    tr = lax.broadcasted_iota(jnp.int32, (C, C), 0)
    tc = lax.broadcasted_iota(jnp.int32, (C, C), 1)
    tril = jnp.where(tc <= tr, 1.0, 0.0).astype(BF16)

    def lanes_of(hp):
        return slice(hp * LANES, (hp + 1) * LANES)

    def shifted(x, prev):
        return jnp.where(first_row, prev, pltpu.roll(x, 1, 0))

    def stack2(x):
        z = jnp.zeros_like(x)
        return jnp.concatenate([jnp.where(head0, x, z), jnp.where(head0, z, x)], axis=0)

    def mm(x, y_stacked):
        return _dot(x.astype(BF16), y_stacked)

    def seg_sum(x):
        return _dot(x.astype(BF16), ones_bd)

    def prep_lora():
        lora_in = []
        for b in range(nb):
            lx = lx_ref[b].astype(F32)
            lx_s = lx + (shifted(lx, pl_ref[b]) - lx) * pv_ref[ROW_MUL:ROW_MUL + 1, 0:LANES]
            lora_in.append(jnp.where(head0, jnp.tanh(lx_s), lx_s).astype(BF16))
            pl_ref[b] = lx[C - 1:C, :]
        lora = _dot(jnp.concatenate(lora_in, axis=0), wl_ref[...])
        for b in range(nb):
            lora_ref[b] = lora[b * C:(b + 1) * C]

    def prep_unit(b, hp):
        sl = lanes_of(hp)
        par = lambda r_: pv_ref[r_:r_ + 1, sl]
        r_raw = r_ref[b, :, sl].astype(F32)
        k_raw = k_ref[b, :, sl].astype(F32)
        v_raw = v_ref[b, :, sl].astype(F32)
        r = r_raw + (shifted(r_raw, pr_ref[b, :, sl]) - r_raw) * par(ROW_MUR)
        k = k_raw + (shifted(k_raw, pk_ref[b, :, sl]) - k_raw) * par(ROW_MUK)
        v = v_raw + (shifted(v_raw, pvp_ref[b, :, sl]) - v_raw) * par(ROW_MUV)
        pr_ref[b, :, sl] = r_raw[C - 1:C, :]
        pk_ref[b, :, sl] = k_raw[C - 1:C, :]
        pvp_ref[b, :, sl] = v_raw[C - 1:C, :]

        logw = (-math.exp(-0.5) * LOG2_E) * _sigmoid(par(ROW_W0) + lora_ref[b, :, sl])
        lr = _sigmoid(par(ROW_A0) + lora_ref[b, :, d_a + hp * LANES:d_a + (hp + 1) * LANES])

        kk = k * par(ROW_KK)
        kmod = k * (1.0 + (lr - 1.0) * par(ROW_KA))
        sums = seg_sum(jnp.concatenate([kk * kk, r * kmod * par(ROW_RK)], axis=0))
        kk = kk * lax.rsqrt(jnp.maximum(sums[:C], 1e-24))
        bvec = kk * lr
        bonus_st[slot_w, b, :, sl] = sums[C:] * v

        p_inc = _split_dot_left(tril, logw)
        p_mid = p_inc[C // 2 - 1: C // 2, :]
        p_end = p_inc[C - 1: C, :]
        pc = p_inc - p_mid
        e_inv = jnp.exp2(-pc)
        e_out = jnp.exp2(p_end - p_inc)
        a_st[slot_w, b, :, sl] = (-kk * jnp.exp2(pc - logw)).astype(BF16)
        r_st[slot_w, b, :, sl] = (r * jnp.exp2(pc)).astype(BF16)
        b_st[slot_w, b, :, sl] = (bvec * e_inv).astype(BF16)
        k_st[slot_w, b, :, sl] = (kmod * e_inv).astype(BF16)
        bd_st[slot_w, b, :, sl] = (bvec * e_out).astype(BF16)
        kd_st[slot_w, b, :, sl] = (kmod * e_out).astype(BF16)
        v_st[slot_w, b, :, sl] = v.astype(BF16)
        wmid_st[slot_w, b, :, sl] = jnp.exp2(p_mid)
        wend_st[slot_w, b, :, sl] = jnp.exp2(p_end)

    env = {}

    def ld(ref, b, hp):
        return ref[slot_r, b, :, lanes_of(hp)]

    def st_scores():
        env["ar"] = [jnp.concatenate([ld(a_st, b, hp), ld(r_st, b, hp)], axis=0) for b, hp in units]
        bk = [jnp.concatenate([stack2(ld(b_st, b, hp)), stack2(ld(k_st, b, hp))], axis=0)
              for b, hp in units]
        env["a_all"] = [_dot_nt(env["ar"][u], bk[u]) for u in range(n_units)]

    def st_state_in():
        env["s_prev"] = [s_ref[u] for u in range(n_units)]
        s_hat = [(env["s_prev"][u] * ld(wmid_st, b, hp)).astype(BF16)
                 for u, (b, hp) in enumerate(units)]
        env["ar_s"] = [_dot_nt(env["ar"][u], s_hat[u]) for u in range(n_units)]
        a_all = env["a_all"]
        env["a_ab"] = [jnp.where(strict, a[:C, :LANES], 0.0) for a in a_all]
        env["a_rb"] = [jnp.where(incl, a[C:, :LANES], 0.0) for a in a_all]
        a_k = [jnp.concatenate([jnp.where(strict, a[:C, LANES:], 0.0),
                                jnp.where(incl, a[C:, LANES:], 0.0)], axis=0) for a in a_all]
        env["a_v"] = [mm(a_k[u], stack2(ld(v_st, b, hp))) for u, (b, hp) in enumerate(units)]
        env["rhs"] = [env["ar_s"][u][:C] + env["a_v"][u][:C] for u in range(n_units)]

    def diag_rhs(x_c):
        return jnp.where(same_sub_sq, jnp.concatenate([x_c] * (LANES // SUB), axis=0),
                         0.0).astype(BF16)

    def fused_step(p, l_pow):
        x = mm(jnp.concatenate([p, l_pow], axis=0), diag_rhs(l_pow))
        return p + x[:SUB], x[SUB:]

    def st_t1():
        a_ab = env["a_ab"]
        l_d = [jnp.where(same_sub, a, 0.0) for a in a_ab]
        env["l_o"] = [stack2((a_ab[u] - l_d[u]).astype(BF16)) for u in range(n_units)]
        l_c = [sum(a[i * SUB:(i + 1) * SUB] for i in range(1, C // SUB)) + a[:SUB] for a in l_d]
        env["l_pow"] = [mm(l_c[u], diag_rhs(l_c[u])) for u in range(n_units)]
        env["p"] = [eye_c + a for a in l_c]

    def st_t2():
        res = [fused_step(env["p"][u], env["l_pow"][u]) for u in range(n_units)]
        env["p"] = [r_[0] for r_ in res]
        env["l_pow"] = [r_[1] for r_ in res]

    def st_t4():
        p, l8 = env["p"], env["l_pow"]
        t_c = [p[u] + mm(p[u], diag_rhs(l8[u])) for u in range(n_units)]
        env["t_d"] = [jnp.where(same_sub, jnp.concatenate([t] * (C // SUB), axis=0), 0.0)
                      for t in t_c]

    def st_n1():
        env["n_pow"] = [mm(env["t_d"][u], env["l_o"][u]) for u in range(n_units)]
        env["q"] = [eye + n for n in env["n_pow"]]

    def st_n_square():
        n = env["n_pow"]
        env["n_pow"] = [mm(n[u], stack2(n[u].astype(BF16))) for u in range(n_units)]

    def st_q():
        q, n = env["q"], env["n_pow"]
        env["q"] = [q[u] + mm(q[u], stack2(n[u].astype(BF16))) for u in range(n_units)]

    def st_tinv():
        env["t_inv"] = [mm(env["q"][u], stack2(env["t_d"][u].astype(BF16))) for u in range(n_units)]

    def st_u():
        env["u_b"] = [mm(env["t_inv"][u], stack2(env["rhs"][u].astype(BF16))).astype(BF16)
                      for u in range(n_units)]

    def st_y_state():
        u_b = env["u_b"]
        env["y"] = [env["ar_s"][u][C:] + env["a_v"][u][C:] + mm(env["a_rb"][u], stack2(u_b[u]))
                    for u in range(n_units)]
        for u, (b, hp) in enumerate(units):
            uv = jnp.concatenate([u_b[u], ld(v_st, b, hp)], axis=0)
            bk_d = jnp.concatenate([ld(bd_st, b, hp), ld(kd_st, b, hp)], axis=0)
            s_new = env["s_prev"][u] * ld(wend_st, b, hp) + _dot_tn(uv, bk_d)
            s_ref[u] = jnp.where(same_head, s_new, 0.0)

    def st_mean():
        y = jnp.concatenate(env["y"], axis=0)
        env["yc"] = y - _dot(y.astype(BF16), mean_bd)

    def st_out():
        yc = env["yc"]
        var = _dot((yc * yc).astype(BF16), mean_bd)
        yn_all = yc * lax.rsqrt(var + GN_EPS)
        for u, (b, hp) in enumerate(units):
            sl = lanes_of(hp)
            yn = yn_all[u * C:(u + 1) * C] * pv_ref[ROW_LNG:ROW_LNG + 1, sl] + pv_ref[ROW_LNB:ROW_LNB + 1, sl]
            z = z_ref[b, :, sl].astype(F32)
            o_ref[b, :, sl] = ((yn + ld(bonus_st, b, hp)) * (z * _sigmoid(z))).astype(BF16)

    n_diag = SUB.bit_length() - 3
    n_rem = (C // SUB).bit_length() - 2
    stages = ([st_scores, st_state_in, st_t1] + [st_t2] * n_diag + [st_t4, st_n1]
              + [st_n_square, st_q] * n_rem + [st_tinv, st_u, st_y_state, st_mean, st_out])

    prep_lora()
    per_stage = -(-n_units // (len(stages) - 1))
    todo = list(units)
    for stage in stages:
        stage()
        for b, hp in todo[:per_stage]:
            prep_unit(b, hp)
        todo = todo[per_stage:]
    assert not todo


def _wkv(proj3, lx3, pvec, wl, layer, d_a, nb):
    bsz, seq, _ = proj3.shape
    C = CHUNK
    n_chunks = seq // C
    n_pairs = d_a // LANES
    cur = lambda idx: (lambda b, s: (b, jnp.minimum(s, n_chunks - 1), idx))
    prev = lambda idx: (lambda b, s: (b, jnp.maximum(s - 1, 0), idx))
    stash = lambda dt: pltpu.VMEM((2, nb, C, d_a), dt)
    return pl.pallas_call(
        _wkv_kernel,
        name="wkv7_chunked",
        grid=(bsz // nb, n_chunks + 1),
        in_specs=[
            pl.BlockSpec((nb, C, d_a), cur(9)),
            pl.BlockSpec((nb, C, d_a), cur(10)),
            pl.BlockSpec((nb, C, d_a), cur(11)),
            pl.BlockSpec((nb, C, LANES), cur(0)),
            pl.BlockSpec((nb, C, d_a), prev(8)),
            pl.BlockSpec((None, N_ROWS, d_a), lambda b, s: (layer, 0, 0)),
            pl.BlockSpec((None, LANES, 2 * d_a), lambda b, s: (layer, 0, 0)),
        ],
        out_specs=pl.BlockSpec((nb, C, d_a), prev(0)),
        out_shape=jax.ShapeDtypeStruct((bsz, seq, d_a), BF16),
        scratch_shapes=[
            pltpu.VMEM((nb * n_pairs, LANES, LANES), F32),
            pltpu.VMEM((nb, 1, d_a), F32),
            pltpu.VMEM((nb, 1, d_a), F32),
            pltpu.VMEM((nb, 1, d_a), F32),
            pltpu.VMEM((nb, 1, LANES), F32),
            pltpu.VMEM((nb, C, 2 * d_a), F32),
            stash(BF16), stash(BF16), stash(BF16), stash(BF16), stash(BF16), stash(BF16),
            stash(BF16), stash(F32),
            pltpu.VMEM((2, nb, 1, d_a), F32), pltpu.VMEM((2, nb, 1, d_a), F32),
        ],
        compiler_params=pltpu.CompilerParams(
            dimension_semantics=("parallel", "arbitrary"), vmem_limit_bytes=VMEM_LIMIT),
    )(proj3, proj3, proj3, lx3, proj3, pvec, wl)


def _out_kernel(x_ref, ya_ref, ga_ref, gb_ref, bg_ref, cg_ref, hb_ref, zb_ref, cgh_ref, hbh_ref,
                mod_ref, g_ref, pv_ref, pa_ref, pb_ref, wo_ref, o_ref, *, tiles_per_seq):
    tm = x_ref.shape[0]
    i = pl.program_id(0)
    starts_seq = (i % tiles_per_seq) == 0

    u = cg_ref[...].astype(F32) * hb_ref[...].astype(F32)
    halo = cgh_ref[...].astype(F32) * hbh_ref[...].astype(F32)
    halo = jnp.where(starts_seq, 0.0, halo)
    row = lax.broadcasted_iota(jnp.int32, u.shape, 0)
    u1 = jnp.where(row == 0, halo[7:8, :], pltpu.roll(u, 1, 0))
    u2 = jnp.where(row == 0, halo[6:7, :], jnp.where(row == 1, halo[7:8, :], pltpu.roll(u, 2, 0)))
    conv = (pv_ref[ROW_CW0:ROW_CW0 + 1, :] * u2 + pv_ref[ROW_CW1:ROW_CW1 + 1, :] * u1
            + pv_ref[ROW_CW2:ROW_CW2 + 1, :] * u)
    zb = zb_ref[...].astype(F32)
    yb_in = bg_ref[...].astype(F32) * conv * (zb * _sigmoid(zb))

    y_a = _dot(ya_ref[...], pa_ref[...])
    y_b = _dot(yb_in.astype(BF16), pb_ref[...])
    m = _sigmoid(ga_ref[...].astype(F32)) * y_a + _sigmoid(gb_ref[...].astype(F32)) * y_b
    o = _dot(m.astype(BF16), wo_ref[...])
    ms = jnp.mean(o * o, axis=-1, keepdims=True)
    o_n = o * lax.rsqrt(ms + RMS_EPS) * g_ref[1:2, :]
    o_ref[...] = x_ref[...] + mod_ref[2:3, :] * o_n


def _out_stage(x2, ya2, proj, mod4, gvec, pvec, p_a, p_b, w_out, layer, seq, tm, d_a):
    bt, d = x2.shape
    tiles_per_seq = seq // tm
    hb8 = tm // 8
    const = lambda i: (layer, 0, 0)
    colblk = lambda idx: (lambda i: (i, idx))
    halo = lambda idx: (lambda i: (jnp.maximum(i * hb8 - 1, 0), idx))
    single = pl.Buffered(1)
    return pl.pallas_call(
        functools.partial(_out_kernel, tiles_per_seq=tiles_per_seq),
        name="out_stage",
        grid=(bt // tm,),
        in_specs=[
            pl.BlockSpec((tm, d), lambda i: (i, 0)),
            pl.BlockSpec((tm, d_a), lambda i: (i, 0)),
            pl.BlockSpec((tm, d), colblk(0)),
            pl.BlockSpec((tm, d), colblk(1)),
            pl.BlockSpec((tm, d_a), colblk(4)),
            pl.BlockSpec((tm, d_a), colblk(5)),
            pl.BlockSpec((tm, d_a), colblk(6)),
            pl.BlockSpec((tm, d_a), colblk(7)),
            pl.BlockSpec((8, d_a), halo(5)),
            pl.BlockSpec((8, d_a), halo(6)),
            pl.BlockSpec((None, None, 3, d), lambda i: (layer, i // tiles_per_seq, 0, 0)),
            pl.BlockSpec((None, 8, d), const),
            pl.BlockSpec((None, N_ROWS, d_a), const),
            pl.BlockSpec((None, d_a, d), const, pipeline_mode=single),
            pl.BlockSpec((None, d_a, d), const, pipeline_mode=single),
            pl.BlockSpec((None, d, d), const, pipeline_mode=single),
        ],
        out_specs=pl.BlockSpec((tm, d), lambda i: (i, 0)),
        out_shape=jax.ShapeDtypeStruct((bt, d), F32),
        compiler_params=pltpu.CompilerParams(
            dimension_semantics=("parallel",), vmem_limit_bytes=VMEM_LIMIT),
    )(x2, ya2, proj, proj, proj, proj, proj, proj, proj, proj, mod4, gvec, pvec, p_a, p_b, w_out)


def _tiles(bsz, seq, d, d_a):
    f32, bf16 = 4, 2
    proj_tm = min(1024, seq)
    proj_tn = min(1024, d_a)
    proj_bytes = (2 * proj_tm * d * f32 + proj_tm * d * bf16
                  + 2 * d * proj_tn * f32 + d * proj_tn * bf16
                  + 2 * proj_tm * proj_tn * bf16)
    out_tm = min(256, seq)
    out_bytes = ((d_a * d * 2 + d * d) * bf16
                 + 2 * out_tm * (2 * d * f32 + (5 * d_a + 2 * d) * bf16)
                 + 5 * out_tm * d * f32)
    assert max(proj_bytes, out_bytes) < VMEM_LIMIT
    wkv_rows = 4 if bsz % 4 == 0 else (2 if bsz % 2 == 0 else 1)
    return dict(proj_tm=proj_tm, proj_tn=proj_tn, out_tm=out_tm, wkv_rows=wkv_rows)


def kernel(x, c, ada_w, ada_b, pre_gain, post_gain, w_in, mu_shift, w0, w2, a0, a2, k_k, k_a, r_k,
           lnx_gain, lnx_bias, conv_w, p_a, p_b, w_out):
    bsz, seq, d = x.shape
    depth = ada_w.shape[0]
    d_a = w0.shape[-1]
    assert d == 2 * d_a and d_a % LANES == 0 and seq % CHUNK == 0 and 2 * LORA == LANES
    assert w2.shape[1] == LORA and a2.shape[1] == LORA

    pad_l = jnp.zeros((depth, d_a - 2 * LORA), F32)
    rows = [w0, a0, k_k, k_a, r_k.reshape(depth, d_a), lnx_gain, lnx_bias,
            mu_shift[:, :d_a], mu_shift[:, d_a:2 * d_a], mu_shift[:, 2 * d_a:3 * d_a],
            jnp.concatenate([mu_shift[:, 3 * d_a:], pad_l], axis=-1),
            conv_w[:, 0], conv_w[:, 1], conv_w[:, 2]]
    rows += [jnp.zeros((depth, d_a), F32)] * (N_ROWS - len(rows))
    pvec = jnp.stack(rows, axis=1)
    gvec = jnp.stack([pre_gain, post_gain] + [jnp.zeros_like(pre_gain)] * 6, axis=1)
    zl = jnp.zeros((depth, LORA, d_a), F32)
    wl = jnp.concatenate([jnp.concatenate([w2, zl], axis=-1),
                          jnp.concatenate([zl, a2], axis=-1)], axis=1).astype(BF16)

    c_rows = 8 * pl.cdiv(bsz, 8)
    c_pad = jnp.zeros((c_rows, d), F32).at[:bsz].set(c)
    t = _tiles(bsz, seq, d, d_a)
    mod, p_a_b, p_b_b, w_out_b = _ada_mod(c_pad, ada_w, ada_b.reshape(depth, 1, 3 * d),
                                          p_a, p_b, w_out)
    mod4 = mod[:, :bsz].reshape(depth, bsz, 3, d)

    x2 = x.reshape(bsz * seq, d)
    for layer in range(depth):
        proj, lx = _proj(x2, mod4, gvec, w_in, layer, seq, t["proj_tm"], t["proj_tn"], d_a)
        ya = _wkv(proj.reshape(bsz, seq, -1), lx.reshape(bsz, seq, -1), pvec, wl, layer, d_a,
                  t["wkv_rows"])
        x2 = _out_stage(x2, ya.reshape(bsz * seq, d_a), proj, mod4, gvec, pvec,
                        p_a_b, p_b_b, w_out_b, layer, seq, t["out_tm"], d_a)
    return x2.reshape(bsz, seq, d)
```

```python
import functools
import math

import jax
import jax.numpy as jnp
from jax import lax
from jax.experimental import pallas as pl
from jax.experimental.pallas import tpu as pltpu

HEAD_SIZE = 64
LORA = 64
RMS_EPS = 1e-6
GN_EPS = 64e-5
CHUNK = 64
SUB = 32
LANES = 128
VMEM_LIMIT = 56 * 1024 * 1024
LOG2_E = math.log2(math.e)

F32 = jnp.float32
BF16 = jnp.bfloat16

(ROW_W0, ROW_A0, ROW_KK, ROW_KA, ROW_RK, ROW_LNG, ROW_LNB, ROW_MUR, ROW_MUK, ROW_MUV,
 ROW_MUL, ROW_CW0, ROW_CW1, ROW_CW2) = range(14)
N_ROWS = 16


def _dot(a, b):
    return jnp.dot(a, b, preferred_element_type=F32)


def _dot_nt(a, b):
    return lax.dot_general(a, b, (((1,), (1,)), ((), ())), preferred_element_type=F32)


def _dot_tn(a, b):
    return lax.dot_general(a, b, (((0,), (0,)), ((), ())), preferred_element_type=F32)


def _split_dot(x, m):
    hi = x.astype(BF16)
    lo = (x - hi.astype(F32)).astype(BF16)
    return _dot(hi, m) + _dot(lo, m)


def _split_dot_left(m, x):
    hi = x.astype(BF16)
    lo = (x - hi.astype(F32)).astype(BF16)
    return _dot(m, hi) + _dot(m, lo)


def _sigmoid(x):
    return jax.nn.sigmoid(x)


def _ada_kernel(c_ref, w_ref, b_ref, o_ref):
    c = c_ref[...]
    c_act = c * _sigmoid(c)
    w = w_ref[...]
    w_hi = w.astype(BF16)
    w_lo = (w - w_hi.astype(F32)).astype(BF16)
    o_ref[...] = _split_dot(c_act, w_hi) + _dot(c_act.astype(BF16), w_lo) + b_ref[...]


def _ada_mod(c_pad, ada_w, ada_b3, tn):
    depth, d, n3 = ada_w.shape
    rows = c_pad.shape[0]
    return pl.pallas_call(
        _ada_kernel,
        name="ada_mod",
        grid=(depth, pl.cdiv(n3, tn)),
        in_specs=[
            pl.BlockSpec((rows, d), lambda l, j: (0, 0)),
            pl.BlockSpec((None, d, tn), lambda l, j: (l, 0, j)),
            pl.BlockSpec((None, 1, tn), lambda l, j: (l, 0, j)),
        ],
        out_specs=pl.BlockSpec((None, rows, tn), lambda l, j: (l, 0, j)),
        out_shape=jax.ShapeDtypeStruct((depth, rows, n3), F32),
        compiler_params=pltpu.CompilerParams(
            dimension_semantics=("arbitrary", "arbitrary"), vmem_limit_bytes=VMEM_LIMIT),
    )(c_pad, ada_w, ada_b3)


W_BUFS = 3


def _proj_kernel(x_ref, mod_ref, g_ref, w_hbm, wl_ref, o_ref, ol_ref, h_ref, wbuf, sem,
                 *, layer, w_col):
    n_j = pl.num_programs(1)
    total = pl.num_programs(0) * n_j
    step = pl.program_id(0) * n_j + pl.program_id(1)
    tn = wbuf.shape[2]

    def tile_copy(s):
        slot = s % W_BUFS
        return pltpu.make_async_copy(w_hbm.at[layer, :, pl.ds(w_col(s % n_j), tn)],
                                     wbuf.at[slot], sem.at[slot])

    @pl.when(step == 0)
    def _():
        for s in range(W_BUFS - 1):
            tile_copy(jnp.int32(s)).start()

    @pl.when(step + (W_BUFS - 1) < total)
    def _():
        tile_copy(step + (W_BUFS - 1)).start()

    @pl.when(pl.program_id(1) == 0)
    def _():
        x = x_ref[...]
        ms = jnp.mean(x * x, axis=-1, keepdims=True)
        y = x * lax.rsqrt(ms + RMS_EPS) * g_ref[0:1, :]
        h = (y * (1.0 + mod_ref[1:2, :]) + mod_ref[0:1, :]).astype(BF16)
        h_ref[...] = h
        ol_ref[...] = _dot(h, wl_ref[...].astype(BF16)).astype(BF16)

    tile_copy(step).wait()
    o_ref[...] = _dot(h_ref[...], wbuf[step % W_BUFS].astype(BF16)).astype(BF16)


def _proj(x2, mod4, gvec, w_in, layer, seq, tm, tn, d_a):
    bt, d = x2.shape
    rows_per_seq = seq // tm
    cols_a = 3 * d_a + 2 * LORA
    off_b = cols_a + d_a
    off_gate = off_b + 4 * d_a
    n_main = 12 * d_a
    assert d_a % tn == 0

    def w_col(j):
        new = j * tn
        old = jnp.where(new < 4 * d_a, new + off_gate,
                        jnp.where(new < 8 * d_a, new + (off_b - 4 * d_a),
                                  jnp.where(new < 9 * d_a, new + (cols_a - 8 * d_a),
                                            new - 9 * d_a)))
        return pl.multiple_of(old, LANES)

    assert n_main // tn >= W_BUFS - 1
    return pl.pallas_call(
        functools.partial(_proj_kernel, layer=layer, w_col=w_col),
        name="in_proj",
        grid=(bt // tm, n_main // tn),
        in_specs=[
            pl.BlockSpec((tm, d), lambda i, j: (i, 0)),
            pl.BlockSpec((None, None, 3, d), lambda i, j: (layer, i // rows_per_seq, 0, 0)),
            pl.BlockSpec((None, 8, d), lambda i, j: (layer, 0, 0)),
            pl.BlockSpec(memory_space=pl.ANY),
            pl.BlockSpec((pl.Squeezed(), pl.Element(d), pl.Element(2 * LORA)),
                         lambda i, j: (layer, 0, 3 * d_a)),
        ],
        out_specs=[pl.BlockSpec((tm, tn), lambda i, j: (i, j)),
                   pl.BlockSpec((tm, 2 * LORA), lambda i, j: (i, 0))],
        out_shape=[jax.ShapeDtypeStruct((bt, n_main), BF16),
                   jax.ShapeDtypeStruct((bt, 2 * LORA), BF16)],
        scratch_shapes=[pltpu.VMEM((tm, d), BF16), pltpu.VMEM((W_BUFS, d, tn), F32),
                        pltpu.SemaphoreType.DMA((W_BUFS,))],
        compiler_params=pltpu.CompilerParams(
            dimension_semantics=("arbitrary", "arbitrary"), vmem_limit_bytes=VMEM_LIMIT),
    )(x2, mod4, gvec, w_in, w_in)


def _wkv_kernel(r_ref, k_ref, v_ref, lx_ref, z_ref, pv_ref, wl_ref, o_ref,
                s_ref, pr_ref, pk_ref, pvp_ref, pl_ref, lora_ref,
                a_st, r_st, b_st, k_st, bd_st, kd_st, v_st, bonus_st, wmid_st, wend_st):
    step = pl.program_id(1)
    nb, C, d_a = r_ref.shape
    n_pairs = d_a // LANES
    units = [(b, hp) for b in range(nb) for hp in range(n_pairs)]
    n_units = len(units)
    slot_w = step % 2
    slot_r = 1 - slot_w

    @pl.when(step == 0)
    def _():
        s_ref[...] = jnp.zeros_like(s_ref)
        pr_ref[...] = jnp.zeros_like(pr_ref)
        pk_ref[...] = jnp.zeros_like(pk_ref)
        pvp_ref[...] = jnp.zeros_like(pvp_ref)
        pl_ref[...] = jnp.zeros_like(pl_ref)
        for ref in (a_st, r_st, b_st, k_st, bd_st, kd_st, v_st, bonus_st, wmid_st, wend_st):
            ref[1] = jnp.zeros(ref.shape[1:], ref.dtype)

    row = lax.broadcasted_iota(jnp.int32, (C, LANES), 0)
    lane = lax.broadcasted_iota(jnp.int32, (C, LANES), 1)
    col = lane & (HEAD_SIZE - 1)
    head0 = lane < HEAD_SIZE
    strict = col < row
    incl = col <= row
    same_sub = (col // SUB) == (row // SUB)
    eye = jnp.where(col == row, 1.0, 0.0).astype(F32)
    eye_c = jnp.where((lax.broadcasted_iota(jnp.int32, (SUB, LANES), 1) & (SUB - 1))
                      == lax.broadcasted_iota(jnp.int32, (SUB, LANES), 0), 1.0, 0.0).astype(F32)
    first_row = row == 0

    sq_r = lax.broadcasted_iota(jnp.int32, (LANES, LANES), 0)
    sq_c = lax.broadcasted_iota(jnp.int32, (LANES, LANES), 1)
    same_head = (sq_r < HEAD_SIZE) == (sq_c < HEAD_SIZE)
    same_sub_sq = (sq_r // SUB) == (sq_c // SUB)
    ones_bd = jnp.where(same_head, 1.0, 0.0).astype(BF16)
    mean_bd = jnp.where(same_head, 1.0 / HEAD_SIZE, 0.0).astype(BF16)
    tr = lax.broadcasted_iota(jnp.int32, (C, C), 0)
    tc = lax.broadcasted_iota(jnp.int32, (C, C), 1)
    tril = jnp.where(tc <= tr, 1.0, 0.0).astype(BF16)

    def lanes_of(hp):
        return slice(hp * LANES, (hp + 1) * LANES)

    def shifted(x, prev):
        return jnp.where(first_row, prev, pltpu.roll(x, 1, 0))

    def stack2(x):
        z = jnp.zeros_like(x)
        return jnp.concatenate([jnp.where(head0, x, z), jnp.where(head0, z, x)], axis=0)

    def mm(x, y_stacked):
        return _dot(x.astype(BF16), y_stacked)

    def seg_sum(x):
        return _dot(x.astype(BF16), ones_bd)

    def prep_lora():
        lora_in = []
        for b in range(nb):
            lx = lx_ref[b].astype(F32)
            lx_s = lx + (shifted(lx, pl_ref[b]) - lx) * pv_ref[ROW_MUL:ROW_MUL + 1, 0:LANES]
            lora_in.append(jnp.where(head0, jnp.tanh(lx_s), lx_s).astype(BF16))
            pl_ref[b] = lx[C - 1:C, :]
        lora = _dot(jnp.concatenate(lora_in, axis=0), wl_ref[...])
        for b in range(nb):
            lora_ref[b] = lora[b * C:(b + 1) * C]

    def prep_unit(b, hp):
        sl = lanes_of(hp)
        par = lambda r_: pv_ref[r_:r_ + 1, sl]
        r_raw = r_ref[b, :, sl].astype(F32)
        k_raw = k_ref[b, :, sl].astype(F32)
        v_raw = v_ref[b, :, sl].astype(F32)
        r = r_raw + (shifted(r_raw, pr_ref[b, :, sl]) - r_raw) * par(ROW_MUR)
        k = k_raw + (shifted(k_raw, pk_ref[b, :, sl]) - k_raw) * par(ROW_MUK)
        v = v_raw + (shifted(v_raw, pvp_ref[b, :, sl]) - v_raw) * par(ROW_MUV)
        pr_ref[b, :, sl] = r_raw[C - 1:C, :]
        pk_ref[b, :, sl] = k_raw[C - 1:C, :]
        pvp_ref[b, :, sl] = v_raw[C - 1:C, :]

        logw = (-math.exp(-0.5) * LOG2_E) * _sigmoid(par(ROW_W0) + lora_ref[b, :, sl])
        lr = _sigmoid(par(ROW_A0) + lora_ref[b, :, d_a + hp * LANES:d_a + (hp + 1) * LANES])

        kk = k * par(ROW_KK)
        kmod = k * (1.0 + (lr - 1.0) * par(ROW_KA))
        sums = seg_sum(jnp.concatenate([kk * kk, r * kmod * par(ROW_RK)], axis=0))
        kk = kk * lax.rsqrt(jnp.maximum(sums[:C], 1e-24))
        bvec = kk * lr
        bonus_st[slot_w, b, :, sl] = sums[C:] * v

        p_inc = _split_dot_left(tril, logw)
        p_mid = p_inc[C // 2 - 1: C // 2, :]
        p_end = p_inc[C - 1: C, :]
        pc = p_inc - p_mid
        e_inv = jnp.exp2(-pc)
        e_out = jnp.exp2(p_end - p_inc)
        a_st[slot_w, b, :, sl] = (-kk * jnp.exp2(pc - logw)).astype(BF16)
        r_st[slot_w, b, :, sl] = (r * jnp.exp2(pc)).astype(BF16)
        b_st[slot_w, b, :, sl] = (bvec * e_inv).astype(BF16)
        k_st[slot_w, b, :, sl] = (kmod * e_inv).astype(BF16)
        bd_st[slot_w, b, :, sl] = (bvec * e_out).astype(BF16)
        kd_st[slot_w, b, :, sl] = (kmod * e_out).astype(BF16)
        v_st[slot_w, b, :, sl] = v.astype(BF16)
        wmid_st[slot_w, b, :, sl] = jnp.exp2(p_mid)
        wend_st[slot_w, b, :, sl] = jnp.exp2(p_end)

    env = {}

    def ld(ref, b, hp):
        return ref[slot_r, b, :, lanes_of(hp)]

    def st_scores():
        env["ar"] = [jnp.concatenate([ld(a_st, b, hp), ld(r_st, b, hp)], axis=0) for b, hp in units]
        bk = [jnp.concatenate([stack2(ld(b_st, b, hp)), stack2(ld(k_st, b, hp))], axis=0)
              for b, hp in units]
        env["a_all"] = [_dot_nt(env["ar"][u], bk[u]) for u in range(n_units)]

    def st_state_in():
        env["s_prev"] = [s_ref[u] for u in range(n_units)]
        s_hat = [(env["s_prev"][u] * ld(wmid_st, b, hp)).astype(BF16)
                 for u, (b, hp) in enumerate(units)]
        env["ar_s"] = [_dot_nt(env["ar"][u], s_hat[u]) for u in range(n_units)]
        a_all = env["a_all"]
        env["a_ab"] = [jnp.where(strict, a[:C, :LANES], 0.0) for a in a_all]
        env["a_rb"] = [jnp.where(incl, a[C:, :LANES], 0.0) for a in a_all]
        a_k = [jnp.concatenate([jnp.where(strict, a[:C, LANES:], 0.0),
                                jnp.where(incl, a[C:, LANES:], 0.0)], axis=0) for a in a_all]
        env["a_v"] = [mm(a_k[u], stack2(ld(v_st, b, hp))) for u, (b, hp) in enumerate(units)]
        env["rhs"] = [env["ar_s"][u][:C] + env["a_v"][u][:C] for u in range(n_units)]

    def diag_rhs(x_c):
        return jnp.where(same_sub_sq, jnp.concatenate([x_c] * (LANES // SUB), axis=0),
                         0.0).astype(BF16)

    def fused_step(p, l_pow):
        x = mm(jnp.concatenate([p, l_pow], axis=0), diag_rhs(l_pow))
        return p + x[:SUB], x[SUB:]

    def st_t1():
        a_ab = env["a_ab"]
        l_d = [jnp.where(same_sub, a, 0.0) for a in a_ab]
        env["l_o"] = [stack2((a_ab[u] - l_d[u]).astype(BF16)) for u in range(n_units)]
        l_c = [sum(a[i * SUB:(i + 1) * SUB] for i in range(1, C // SUB)) + a[:SUB] for a in l_d]
        env["l_pow"] = [mm(l_c[u], diag_rhs(l_c[u])) for u in range(n_units)]
        env["p"] = [eye_c + a for a in l_c]

    def st_t2():
        res = [fused_step(env["p"][u], env["l_pow"][u]) for u in range(n_units)]
        env["p"] = [r_[0] for r_ in res]
        env["l_pow"] = [r_[1] for r_ in res]

    def st_t4():
        p, l8 = env["p"], env["l_pow"]
        t_c = [p[u] + mm(p[u], diag_rhs(l8[u])) for u in range(n_units)]
        env["t_d"] = [jnp.where(same_sub, jnp.concatenate([t] * (C // SUB), axis=0), 0.0)
                      for t in t_c]

    def st_n1():
        env["n_pow"] = [mm(env["t_d"][u], env["l_o"][u]) for u in range(n_units)]
        env["q"] = [eye + n for n in env["n_pow"]]

    def st_n_square():
        n = env["n_pow"]
        env["n_pow"] = [mm(n[u], stack2(n[u].astype(BF16))) for u in range(n_units)]

    def st_q():
        q, n = env["q"], env["n_pow"]
        env["q"] = [q[u] + mm(q[u], stack2(n[u].astype(BF16))) for u in range(n_units)]

    def st_tinv():
        env["t_inv"] = [mm(env["q"][u], stack2(env["t_d"][u].astype(BF16))) for u in range(n_units)]

    def st_u():
        env["u_b"] = [mm(env["t_inv"][u], stack2(env["rhs"][u].astype(BF16))).astype(BF16)
                      for u in range(n_units)]

    def st_y_state():
        u_b = env["u_b"]
        env["y"] = [env["ar_s"][u][C:] + env["a_v"][u][C:] + mm(env["a_rb"][u], stack2(u_b[u]))
                    for u in range(n_units)]
        for u, (b, hp) in enumerate(units):
            uv = jnp.concatenate([u_b[u], ld(v_st, b, hp)], axis=0)
            bk_d = jnp.concatenate([ld(bd_st, b, hp), ld(kd_st, b, hp)], axis=0)
            s_new = env["s_prev"][u] * ld(wend_st, b, hp) + _dot_tn(uv, bk_d)
            s_ref[u] = jnp.where(same_head, s_new, 0.0)

    def st_mean():
        y = jnp.concatenate(env["y"], axis=0)
        env["yc"] = y - _dot(y.astype(BF16), mean_bd)

    def st_out():
        yc = env["yc"]
        var = _dot((yc * yc).astype(BF16), mean_bd)
        yn_all = yc * lax.rsqrt(var + GN_EPS)
        for u, (b, hp) in enumerate(units):
            sl = lanes_of(hp)
            yn = yn_all[u * C:(u + 1) * C] * pv_ref[ROW_LNG:ROW_LNG + 1, sl] + pv_ref[ROW_LNB:ROW_LNB + 1, sl]
            z = z_ref[b, :, sl].astype(F32)
            o_ref[b, :, sl] = ((yn + ld(bonus_st, b, hp)) * (z * _sigmoid(z))).astype(BF16)

    n_diag = SUB.bit_length() - 3
    n_rem = (C // SUB).bit_length() - 2
    stages = ([st_scores, st_state_in, st_t1] + [st_t2] * n_diag + [st_t4, st_n1]
              + [st_n_square, st_q] * n_rem + [st_tinv, st_u, st_y_state, st_mean, st_out])

    prep_lora()
    per_stage = -(-n_units // (len(stages) - 1))
    todo = list(units)
    for stage in stages:
        stage()
        for b, hp in todo[:per_stage]:
            prep_unit(b, hp)
        todo = todo[per_stage:]
    assert not todo


def _wkv(proj3, lx3, pvec, wl, layer, d_a, nb):
    bsz, seq, _ = proj3.shape
    C = CHUNK
    n_chunks = seq // C
    n_pairs = d_a // LANES
    cur = lambda idx: (lambda b, s: (b, jnp.minimum(s, n_chunks - 1), idx))
    prev = lambda idx: (lambda b, s: (b, jnp.maximum(s - 1, 0), idx))
    stash = lambda dt: pltpu.VMEM((2, nb, C, d_a), dt)
    return pl.pallas_call(
        _wkv_kernel,
        name="wkv7_chunked",
        grid=(bsz // nb, n_chunks + 1),
        in_specs=[
            pl.BlockSpec((nb, C, d_a), cur(9)),
            pl.BlockSpec((nb, C, d_a), cur(10)),
            pl.BlockSpec((nb, C, d_a), cur(11)),
            pl.BlockSpec((nb, C, LANES), cur(0)),
            pl.BlockSpec((nb, C, d_a), prev(8)),
            pl.BlockSpec((None, N_ROWS, d_a), lambda b, s: (layer, 0, 0)),
            pl.BlockSpec((None, LANES, 2 * d_a), lambda b, s: (layer, 0, 0)),
        ],
        out_specs=pl.BlockSpec((nb, C, d_a), prev(0)),
        out_shape=jax.ShapeDtypeStruct((bsz, seq, d_a), BF16),
        scratch_shapes=[
            pltpu.VMEM((nb * n_pairs, LANES, LANES), F32),
            pltpu.VMEM((nb, 1, d_a), F32),
            pltpu.VMEM((nb, 1, d_a), F32),
            pltpu.VMEM((nb, 1, d_a), F32),
            pltpu.VMEM((nb, 1, LANES), F32),
            pltpu.VMEM((nb, C, 2 * d_a), F32),
            stash(BF16), stash(BF16), stash(BF16), stash(BF16), stash(BF16), stash(BF16),
            stash(BF16), stash(F32),
            pltpu.VMEM((2, nb, 1, d_a), F32), pltpu.VMEM((2, nb, 1, d_a), F32),
        ],
        compiler_params=pltpu.CompilerParams(
            dimension_semantics=("parallel", "arbitrary"), vmem_limit_bytes=VMEM_LIMIT),
    )(proj3, proj3, proj3, lx3, proj3, pvec, wl)


def _out_kernel(x_ref, ya_ref, ga_ref, gb_ref, bg_ref, cg_ref, hb_ref, zb_ref, cgh_ref, hbh_ref,
                mod_ref, g_ref, pv_ref, pa_ref, pb_ref, wo_ref, o_ref, *, tiles_per_seq):
    tm = x_ref.shape[0]
    i = pl.program_id(0)
    starts_seq = (i % tiles_per_seq) == 0

    u = cg_ref[...].astype(F32) * hb_ref[...].astype(F32)
    halo = cgh_ref[...].astype(F32) * hbh_ref[...].astype(F32)
    halo = jnp.where(starts_seq, 0.0, halo)
    row = lax.broadcasted_iota(jnp.int32, u.shape, 0)
    u1 = jnp.where(row == 0, halo[7:8, :], pltpu.roll(u, 1, 0))
    u2 = jnp.where(row == 0, halo[6:7, :], jnp.where(row == 1, halo[7:8, :], pltpu.roll(u, 2, 0)))
    conv = (pv_ref[ROW_CW0:ROW_CW0 + 1, :] * u2 + pv_ref[ROW_CW1:ROW_CW1 + 1, :] * u1
            + pv_ref[ROW_CW2:ROW_CW2 + 1, :] * u)
    zb = zb_ref[...].astype(F32)
    yb_in = bg_ref[...].astype(F32) * conv * (zb * _sigmoid(zb))

    y_a = _dot(ya_ref[...], pa_ref[...])
    y_b = _dot(yb_in.astype(BF16), pb_ref[...])
    m = _sigmoid(ga_ref[...].astype(F32)) * y_a + _sigmoid(gb_ref[...].astype(F32)) * y_b
    o = _dot(m.astype(BF16), wo_ref[...])
    ms = jnp.mean(o * o, axis=-1, keepdims=True)
    o_n = o * lax.rsqrt(ms + RMS_EPS) * g_ref[1:2, :]
    o_ref[...] = x_ref[...] + mod_ref[2:3, :] * o_n


def _out_stage(x2, ya2, proj, mod4, gvec, pvec, p_a, p_b, w_out, layer, seq, tm, d_a):
    bt, d = x2.shape
    tiles_per_seq = seq // tm
    hb8 = tm // 8
    const = lambda i: (layer, 0, 0)
    colblk = lambda idx: (lambda i: (i, idx))
    halo = lambda idx: (lambda i: (jnp.maximum(i * hb8 - 1, 0), idx))
    single = pl.Buffered(1)
    return pl.pallas_call(
        functools.partial(_out_kernel, tiles_per_seq=tiles_per_seq),
        name="out_stage",
        grid=(bt // tm,),
        in_specs=[
            pl.BlockSpec((tm, d), lambda i: (i, 0)),
            pl.BlockSpec((tm, d_a), lambda i: (i, 0)),
            pl.BlockSpec((tm, d), colblk(0)),
            pl.BlockSpec((tm, d), colblk(1)),
            pl.BlockSpec((tm, d_a), colblk(4)),
            pl.BlockSpec((tm, d_a), colblk(5)),
            pl.BlockSpec((tm, d_a), colblk(6)),
            pl.BlockSpec((tm, d_a), colblk(7)),
            pl.BlockSpec((8, d_a), halo(5)),
            pl.BlockSpec((8, d_a), halo(6)),
            pl.BlockSpec((None, None, 3, d), lambda i: (layer, i // tiles_per_seq, 0, 0)),
            pl.BlockSpec((None, 8, d), const),
            pl.BlockSpec((None, N_ROWS, d_a), const),
            pl.BlockSpec((None, d_a, d), const, pipeline_mode=single),
            pl.BlockSpec((None, d_a, d), const, pipeline_mode=single),
            pl.BlockSpec((None, d, d), const, pipeline_mode=single),
        ],
        out_specs=pl.BlockSpec((tm, d), lambda i: (i, 0)),
        out_shape=jax.ShapeDtypeStruct((bt, d), F32),
        compiler_params=pltpu.CompilerParams(
            dimension_semantics=("parallel",), vmem_limit_bytes=VMEM_LIMIT),
    )(x2, ya2, proj, proj, proj, proj, proj, proj, proj, proj, mod4, gvec, pvec, p_a, p_b, w_out)


def _tiles(bsz, seq, d, d_a):
    f32, bf16 = 4, 2
    proj_tm = min(1024, seq)
    proj_tn = min(1024, d_a)
    proj_bytes = (2 * proj_tm * d * f32 + proj_tm * d * bf16
                  + 3 * d * proj_tn * f32 + d * proj_tn * bf16
                  + 2 * proj_tm * proj_tn * bf16)
    out_tm = min(256, seq)
    out_bytes = ((d_a * d * 2 + d * d) * bf16
                 + 2 * out_tm * (2 * d * f32 + (5 * d_a + 2 * d) * bf16)
                 + 5 * out_tm * d * f32)
    assert max(proj_bytes, out_bytes) < VMEM_LIMIT
    wkv_rows = 4 if bsz % 4 == 0 else (2 if bsz % 2 == 0 else 1)
    return dict(ada_tn=min(1024, 3 * d), proj_tm=proj_tm, proj_tn=proj_tn, out_tm=out_tm,
                wkv_rows=wkv_rows)


def kernel(x, c, ada_w, ada_b, pre_gain, post_gain, w_in, mu_shift, w0, w2, a0, a2, k_k, k_a, r_k,
           lnx_gain, lnx_bias, conv_w, p_a, p_b, w_out):
    bsz, seq, d = x.shape
    depth = ada_w.shape[0]
    d_a = w0.shape[-1]
    assert d == 2 * d_a and d_a % LANES == 0 and seq % CHUNK == 0 and 2 * LORA == LANES
    assert w2.shape[1] == LORA and a2.shape[1] == LORA

    pad_l = jnp.zeros((depth, d_a - 2 * LORA), F32)
    rows = [w0, a0, k_k, k_a, r_k.reshape(depth, d_a), lnx_gain, lnx_bias,
            mu_shift[:, :d_a], mu_shift[:, d_a:2 * d_a], mu_shift[:, 2 * d_a:3 * d_a],
            jnp.concatenate([mu_shift[:, 3 * d_a:], pad_l], axis=-1),
            conv_w[:, 0], conv_w[:, 1], conv_w[:, 2]]
    rows += [jnp.zeros((depth, d_a), F32)] * (N_ROWS - len(rows))
    pvec = jnp.stack(rows, axis=1)
    gvec = jnp.stack([pre_gain, post_gain] + [jnp.zeros_like(pre_gain)] * 6, axis=1)
    zl = jnp.zeros((depth, LORA, d_a), F32)
    wl = jnp.concatenate([jnp.concatenate([w2, zl], axis=-1),
                          jnp.concatenate([zl, a2], axis=-1)], axis=1).astype(BF16)
    p_a_b, p_b_b, w_out_b = p_a.astype(BF16), p_b.astype(BF16), w_out.astype(BF16)

    c_rows = 8 * pl.cdiv(bsz, 8)
    c_pad = jnp.zeros((c_rows, d), F32).at[:bsz].set(c)
    t = _tiles(bsz, seq, d, d_a)
    mod = _ada_mod(c_pad, ada_w, ada_b.reshape(depth, 1, 3 * d), tn=t["ada_tn"])
    mod4 = mod[:, :bsz].reshape(depth, bsz, 3, d)

    x2 = x.reshape(bsz * seq, d)
    for layer in range(depth):
        proj, lx = _proj(x2, mod4, gvec, w_in, layer, seq, t["proj_tm"], t["proj_tn"], d_a)
        ya = _wkv(proj.reshape(bsz, seq, -1), lx.reshape(bsz, seq, -1), pvec, wl, layer, d_a,
                  t["wkv_rows"])
        x2 = _out_stage(x2, ya.reshape(bsz * seq, d_a), proj, mod4, gvec, pvec,
                        p_a_b, p_b_b, w_out_b, layer, seq, t["out_tm"], d_a)
    return x2.reshape(bsz, seq, d)
```
